```python
import jax, jax.numpy as jnp
from jax import lax
import numpy as np

D_MODEL = 1024
BATCH = 8
SEQ = 2048
DEPTH = 2

N_MIXERS = 2
SB_HEADS = 16
SB_HEAD_DIM = D_MODEL // SB_HEADS
Q_BLOCK = 128
D_RNN = (D_MODEL * 5) // 4
LRU_BLOCKS = 10
LRU_BLOCK_W = D_RNN // LRU_BLOCKS
CONV_W = 4
LRU_C = 8.0
D_FF = ((8 * D_MODEL // 3 + 127) // 128) * 128
N_SUB = 3
MACARON_W = 0.5
EPS = 1e-6
N_SB_LAYERS = (DEPTH + 1) // 2
N_LRU_LAYERS = DEPTH // 2

kernel_name = "hybrid_stickbreak_rglru_macaron_adaln"


def rmsnorm(x, g):
    xf = x.astype(jnp.float32)
    inv = lax.rsqrt(jnp.mean(xf * xf, axis=-1, keepdims=True) + EPS)
    return (xf * inv).astype(x.dtype) * g


def sublayer_input(x, g, mod_j):
    shift, scale, gate = mod_j[:, 0], mod_j[:, 1], mod_j[:, 2]
    h = rmsnorm(x, g) * (1 + scale[:, None, :]) + shift[:, None, :]
    return h, (1 + gate)[:, None, :]


def swiglu(h, w_gu, w_down):
    g, u = jnp.split(h @ w_gu, 2, axis=-1)
    return (jax.nn.silu(g) * u) @ w_down


def stick_breaking_attention(h, w_qkv, w_o):
    B, S, _ = h.shape
    qkv = (h @ w_qkv).reshape(B, S, 3, SB_HEADS, SB_HEAD_DIM)
    qkv = jnp.transpose(qkv, (2, 0, 3, 1, 4)).astype(jnp.float32)
    q = qkv[0] * (SB_HEAD_DIM ** -0.5)
    k, v = qkv[1], qkv[2]
    outs = []
    for blk in range(S // Q_BLOCK):
        t0 = blk * Q_BLOCK
        L = t0 + Q_BLOCK
        z = jnp.einsum('bhqd,bhkd->bhqk', q[:, :, t0:L], k[:, :, :L])
        t_idx = t0 + jnp.arange(Q_BLOCK)
        s_idx = jnp.arange(L)
        mask = s_idx[None, :] < t_idx[:, None]
        log_keep = jnp.where(mask, jax.nn.log_sigmoid(-z), 0.0)
        later = lax.cumsum(log_keep, axis=3, reverse=True) - log_keep
        w = jnp.where(mask, jnp.exp(jax.nn.log_sigmoid(z) + later), 0.0)
        outs.append(jnp.einsum('bhqk,bhkd->bhqd', w, v[:, :, :L]))
    o = jnp.concatenate(outs, axis=2)
    o = jnp.transpose(o, (0, 2, 1, 3)).reshape(B, S, D_MODEL).astype(h.dtype)
    return o @ w_o


def _lin_rec_combine(left, right):
    a1, b1 = left
    a2, b2 = right
    return a1 * a2, a2 * b1 + b2


def rglru_block(h, w_in, conv_w, conv_b, w_r, b_r, w_i, b_i, lam, w_out):
    B, S, _ = h.shape
    gate, xb = jnp.split(h @ w_in, 2, axis=-1)
    xp = jnp.pad(xb, ((0, 0), (CONV_W - 1, 0), (0, 0)))
    xc = conv_b + xp[:, 0:S] * conv_w[0]
    for tap in range(1, CONV_W):
        xc = xc + xp[:, tap:tap + S] * conv_w[tap]
    xblk = xc.reshape(B, S, LRU_BLOCKS, LRU_BLOCK_W)
    r = jax.nn.sigmoid(jnp.einsum('bsnk,nkj->bsnj', xblk, w_r).reshape(B, S, D_RNN) + b_r)
    i = jax.nn.sigmoid(jnp.einsum('bsnk,nkj->bsnj', xblk, w_i).reshape(B, S, D_RNN) + b_i)
    log_a = -LRU_C * r.astype(jnp.float32) * jax.nn.softplus(-lam.astype(jnp.float32))
    a = jnp.exp(log_a)
    b = jnp.sqrt(-jnp.expm1(2.0 * log_a)) * (i * xc).astype(jnp.float32)
    _, hs = lax.associative_scan(_lin_rec_combine, (a, b), axis=1)
    y = jax.nn.gelu(gate) * hs.astype(h.dtype)
    return y @ w_out


def setup_inputs(seed: int = 0) -> dict:
    key = jax.random.key(seed)
    ks = jax.random.split(key, 24)
    D = D_MODEL
    f32 = jnp.float32
    nrm = lambda k, shape, s: jax.random.normal(k, shape, f32) * s
    x = nrm(ks[0], (BATCH, SEQ, D), 1.0)
    c = nrm(ks[1], (BATCH, D), 1.0)
    mod_w = nrm(ks[2], (DEPTH, D, N_SUB * 3 * D), 0.2 * D ** -0.5)
    mod_b = nrm(ks[3], (DEPTH, N_SUB * 3 * D), 0.05)
    norm_g = 1.0 + nrm(ks[4], (DEPTH, N_SUB, D), 0.02)
    ffn_w_gu = nrm(ks[5], (DEPTH, 2, D, 2 * D_FF), D ** -0.5)
    ffn_w_down = nrm(ks[6], (DEPTH, 2, D_FF, D), D_FF ** -0.5)
    sb_w_qkv = nrm(ks[7], (N_SB_LAYERS, D, 3 * D), D ** -0.5)
    sb_w_o = nrm(ks[8], (N_SB_LAYERS, D, D), D ** -0.5)
    lru_w_in = nrm(ks[9], (N_LRU_LAYERS, D, 2 * D_RNN), D ** -0.5)
    lru_conv_w = nrm(ks[10], (N_LRU_LAYERS, CONV_W, D_RNN), CONV_W ** -0.5)
    lru_conv_b = nrm(ks[11], (N_LRU_LAYERS, D_RNN), 0.02)
    lru_w_r = nrm(ks[12], (N_LRU_LAYERS, LRU_BLOCKS, LRU_BLOCK_W, LRU_BLOCK_W), LRU_BLOCK_W ** -0.5)
    lru_b_r = nrm(ks[13], (N_LRU_LAYERS, D_RNN), 0.1)
    lru_w_i = nrm(ks[14], (N_LRU_LAYERS, LRU_BLOCKS, LRU_BLOCK_W, LRU_BLOCK_W), LRU_BLOCK_W ** -0.5)
    lru_b_i = nrm(ks[15], (N_LRU_LAYERS, D_RNN), 0.1)
    a_c = jax.random.uniform(ks[16], (N_LRU_LAYERS, D_RNN), f32, 0.9, 0.999)
    a0 = a_c ** (1.0 / LRU_C)
    lru_lambda = jnp.log(a0) - jnp.log1p(-a0)
    lru_w_out = nrm(ks[17], (N_LRU_LAYERS, D_RNN, D), D_RNN ** -0.5)
    final_norm_g = 1.0 + nrm(ks[18], (D,), 0.02)
    return {"x": x, "c": c, "mod_w": mod_w, "mod_b": mod_b, "norm_g": norm_g,
            "ffn_w_gu": ffn_w_gu, "ffn_w_down": ffn_w_down,
            "sb_w_qkv": sb_w_qkv, "sb_w_o": sb_w_o,
            "lru_w_in": lru_w_in, "lru_conv_w": lru_conv_w, "lru_conv_b": lru_conv_b,
            "lru_w_r": lru_w_r, "lru_b_r": lru_b_r, "lru_w_i": lru_w_i, "lru_b_i": lru_b_i,
            "lru_lambda": lru_lambda, "lru_w_out": lru_w_out, "final_norm_g": final_norm_g}


def reference(x, c, mod_w, mod_b, norm_g, ffn_w_gu, ffn_w_down, sb_w_qkv, sb_w_o,
              lru_w_in, lru_conv_w, lru_conv_b, lru_w_r, lru_b_r, lru_w_i, lru_b_i,
              lru_lambda, lru_w_out, final_norm_g):
    B = x.shape[0]
    c_act = jax.nn.silu(c)
    for layer in range(DEPTH):
        mod = (c_act @ mod_w[layer] + mod_b[layer]).reshape(B, N_SUB, 3, D_MODEL)
        h, g = sublayer_input(x, norm_g[layer, 0], mod[:, 0])
        x = x + MACARON_W * g * swiglu(h, ffn_w_gu[layer, 0], ffn_w_down[layer, 0])
        h, g = sublayer_input(x, norm_g[layer, 1], mod[:, 1])
        j = layer // N_MIXERS
        if layer % N_MIXERS == 0:
            y = stick_breaking_attention(h, sb_w_qkv[j], sb_w_o[j])
        else:
            y = rglru_block(h, lru_w_in[j], lru_conv_w[j], lru_conv_b[j], lru_w_r[j], lru_b_r[j],
                            lru_w_i[j], lru_b_i[j], lru_lambda[j], lru_w_out[j])
        x = x + g * y
        h, g = sublayer_input(x, norm_g[layer, 2], mod[:, 2])
        x = x + MACARON_W * g * swiglu(h, ffn_w_gu[layer, 1], ffn_w_down[layer, 1])
    return rmsnorm(x, final_norm_g)
```

```python
import functools

import jax
import jax.numpy as jnp
from jax import lax
from jax.experimental import pallas as pl
from jax.experimental.pallas import tpu as pltpu

SB_HEADS = 16
LRU_C = 8.0
MACARON_W = 0.5
EPS = 1e-6
N_SUB = 3

LANES = 128
MXU_TILE = 256
VMEM_LIMIT_BYTES = 56 * 2 ** 20

F32 = jnp.float32
BF16 = jnp.bfloat16


def _params(n_grid_axes):
    return pltpu.CompilerParams(
        dimension_semantics=("arbitrary",) * n_grid_axes,
        vmem_limit_bytes=VMEM_LIMIT_BYTES,
    )


def _resident(block_shape, index_map):
    return pl.BlockSpec(block_shape, index_map, pipeline_mode=pl.Buffered(1))


def _norm_mod(x, g, shift, scale):
    inv = lax.rsqrt(jnp.mean(x * x, axis=-1, keepdims=True) + EPS)
    return (x * inv) * (g * (1.0 + scale)) + shift


def _mod_rows(mod_ref, sub):
    r = N_SUB * sub
    return mod_ref[r:r + 1, :], mod_ref[r + 1:r + 2, :], mod_ref[r + 2:r + 3, :]


def _mod_kernel(c_ref, w_ref, b_ref, o_ref):
    c = c_ref[...]
    ca = c * jax.nn.sigmoid(c)
    ca2 = jnp.concatenate([ca, ca], axis=0).astype(BF16)
    y = jnp.dot(ca2, w_ref[...].astype(BF16), preferred_element_type=F32)
    o_ref[...] = y[:c.shape[0]] + b_ref[...]


def _mod_call(c, mod_w, mod_b):
    depth, d, n = mod_w.shape
    b = c.shape[0]
    tn = 1024
    assert n % tn == 0
    return pl.pallas_call(
        _mod_kernel,
        out_shape=jax.ShapeDtypeStruct((depth, b, n), F32),
        grid=(depth, n // tn),
        in_specs=[
            pl.BlockSpec((b, d), lambda l, j: (0, 0)),
            pl.BlockSpec((None, d, tn), lambda l, j: (l, 0, j)),
            pl.BlockSpec((None, 1, tn), lambda l, j: (l, 0, j)),
        ],
        out_specs=pl.BlockSpec((None, b, tn), lambda l, j: (l, 0, j)),
        compiler_params=_params(2),
        name="mod",
    )(c, mod_w, mod_b.reshape(depth, 1, n))


def _ffn_kernel(x_ref, mod_ref, g_ref, wg_ref, wu_ref, wd_ref, *rest, sub, final):
    if final:
        fg_ref, o_ref, h_scr, act_scr = rest
    else:
        o_ref, h_scr, act_scr = rest
    x = x_ref[...]
    shift, scale, gate = _mod_rows(mod_ref, sub)
    h_scr[...] = _norm_mod(x, g_ref[sub:sub + 1, :], shift, scale).astype(BF16)
    n_chunks = wg_ref.shape[0]

    def chunk(c, carry):
        h = h_scr[...]
        g = jnp.dot(h, wg_ref[c], preferred_element_type=F32)
        u = jnp.dot(h, wu_ref[c], preferred_element_type=F32)
        act_scr[c] = (g * jax.nn.sigmoid(g) * u).astype(BF16)
        return carry

    lax.fori_loop(0, n_chunks, chunk, 0)
    act = jnp.concatenate([act_scr[c] for c in range(n_chunks)], axis=1)
    y = jnp.dot(act, wd_ref[...], preferred_element_type=F32)
    out = x + (MACARON_W * (1.0 + gate)) * y
    if final:
        inv = lax.rsqrt(jnp.mean(out * out, axis=-1, keepdims=True) + EPS)
        out = (out * inv) * fg_ref[...]
    o_ref[...] = out


def _ffn_call(x, mod_l, norm_g_l, w_gu, w_down, sub, final_g=None):
    b, s, d = x.shape
    d_ff = w_down.shape[0]
    tm = 512
    tf = MXU_TILE
    assert s % tm == 0 and d_ff % tf == 0
    nc = d_ff // tf
    wg = w_gu[:, :d_ff].reshape(d, nc, tf).transpose(1, 0, 2).astype(BF16)
    wu = w_gu[:, d_ff:].reshape(d, nc, tf).transpose(1, 0, 2).astype(BF16)
    wd = w_down.astype(BF16)
    final = final_g is not None
    in_specs = [
        pl.BlockSpec((None, tm, d), lambda bi, i: (bi, i, 0)),
        pl.BlockSpec((None, N_SUB * 3, d), lambda bi, i: (bi, 0, 0)),
        pl.BlockSpec((N_SUB, d), lambda bi, i: (0, 0)),
        _resident((nc, d, tf), lambda bi, i: (0, 0, 0)),
        _resident((nc, d, tf), lambda bi, i: (0, 0, 0)),
        _resident((d_ff, d), lambda bi, i: (0, 0)),
    ]
    args = [x, mod_l, norm_g_l, wg, wu, wd]
    if final:
        in_specs.append(pl.BlockSpec((1, d), lambda bi, i: (0, 0)))
        args.append(final_g.reshape(1, d))
    return pl.pallas_call(
        functools.partial(_ffn_kernel, sub=sub, final=final),
        out_shape=jax.ShapeDtypeStruct((b, s, d), F32),
        grid=(b, s // tm),
        in_specs=in_specs,
        out_specs=pl.BlockSpec((None, tm, d), lambda bi, i: (bi, i, 0)),
        scratch_shapes=[pltpu.VMEM((tm, d), BF16), pltpu.VMEM((nc, tm, tf), BF16)],
        compiler_params=_params(2),
        name="ffn_final" if final else "ffn",
    )(*args)


def _qkv_kernel(x_ref, mod_ref, g_ref, w_ref, qt_ref, k_ref, vt_ref, *, sub, scale_q):
    d = x_ref.shape[-1]
    shift, scale, _ = _mod_rows(mod_ref, sub)
    h = _norm_mod(x_ref[...], g_ref[sub:sub + 1, :], shift, scale).astype(BF16)
    q = jnp.dot(h, w_ref[:, 0:d], preferred_element_type=F32) * scale_q
    qt_ref[...] = q.T.astype(BF16)
    k_ref[...] = jnp.dot(h, w_ref[:, d:2 * d], preferred_element_type=F32).astype(BF16)
    v = jnp.dot(h, w_ref[:, 2 * d:3 * d], preferred_element_type=F32)
    tk = vt_ref.shape[-1]
    for j in range(vt_ref.shape[0]):
        vt_ref[j] = v[j * tk:(j + 1) * tk, :].T.astype(BF16)


def _qkv_call(x, mod_l, norm_g_l, w_qkv, sub, tk):
    b, s, d = x.shape
    tm = 512
    assert s % tm == 0 and tm % tk == 0
    dh = d // SB_HEADS
    return pl.pallas_call(
        functools.partial(_qkv_kernel, sub=sub, scale_q=dh ** -0.5),
        out_shape=(
            jax.ShapeDtypeStruct((b, d, s), BF16),
            jax.ShapeDtypeStruct((b, s, d), BF16),
            jax.ShapeDtypeStruct((b, s // tk, d, tk), BF16),
        ),
        grid=(b, s // tm),
        in_specs=[
            pl.BlockSpec((None, tm, d), lambda bi, i: (bi, i, 0)),
            pl.BlockSpec((None, N_SUB * 3, d), lambda bi, i: (bi, 0, 0)),
            pl.BlockSpec((N_SUB, d), lambda bi, i: (0, 0)),
            _resident((d, 3 * d), lambda bi, i: (0, 0)),
        ],
        out_specs=(
            pl.BlockSpec((None, d, tm), lambda bi, i: (bi, 0, i)),
            pl.BlockSpec((None, tm, d), lambda bi, i: (bi, i, 0)),
            pl.BlockSpec((None, tm // tk, d, tk), lambda bi, i: (bi, i, 0, 0)),
        ),
        compiler_params=_params(2),
        name="qkv",
    )(x, mod_l, norm_g_l, w_qkv.astype(BF16))


def _attn_kernel(qt_ref, k_ref, vt_ref, ot_ref, *, dh):
    tq = qt_ref.shape[-1]
    tk = vt_ref.shape[-1]
    qb = pl.program_id(2)
    qt = qt_ref[...]
    frow = lax.broadcasted_iota(jnp.int32, qt.shape, 0)
    zero = jnp.zeros_like(qt)
    qt_heads = (jnp.where(frow < dh, qt, zero), jnp.where(frow >= dh, qt, zero))
    kr = lax.broadcasted_iota(jnp.int32, (tk, tk), 0)
    kc = lax.broadcasted_iota(jnp.int32, (tk, tk), 1)
    later_mat = (kc > kr).astype(BF16)
    causal = (lax.broadcasted_iota(jnp.int32, (tk, tq), 0)
              < lax.broadcasted_iota(jnp.int32, (tk, tq), 1))

    def tile(kb, carry, masked):
        acc, tails = carry
        k_t = k_ref[pl.ds(pl.multiple_of(kb * tk, tk), tk), :]
        vt_t = vt_ref[kb]
        parts, new_tails = [], []
        for hd in range(2):
            z = jnp.dot(k_t, qt_heads[hd], preferred_element_type=F32)
            log_beta = jnp.minimum(z, 0.0) - jnp.log(1.0 + jnp.exp(-jnp.abs(z)))
            log_keep = log_beta - z
            if masked:
                log_keep = jnp.where(causal, log_keep, 0.0)
            later = jnp.dot(later_mat, log_keep.astype(BF16), preferred_element_type=F32)
            w = jnp.exp(log_beta + later)
            if masked:
                w = jnp.where(causal, w, 0.0)
            pv = jnp.dot(vt_t, w.astype(BF16), preferred_element_type=F32)
            parts.append(pv[hd * dh:(hd + 1) * dh, :] * jnp.exp(tails[hd]))
            new_tails.append(tails[hd] + later[0:1, :] + log_keep[0:1, :])
        return acc + jnp.concatenate(parts, axis=0), tuple(new_tails)

    init = (jnp.zeros((2 * dh, tq), F32), (jnp.zeros((1, tq), F32), jnp.zeros((1, tq), F32)))
    carry = tile(qb, init, True)
    carry = lax.fori_loop(0, qb, lambda i, c: tile(qb - 1 - i, c, False), carry)
    ot_ref[...] = carry[0].astype(BF16)


def _attn_call(qt, k, vt):
    b, d, s = qt.shape
    tk = vt.shape[-1]
    tq = tk
    dh = d // SB_HEADS
    hp = 2 * dh
    assert hp == LANES and s % tq == 0
    return pl.pallas_call(
        functools.partial(_attn_kernel, dh=dh),
        out_shape=jax.ShapeDtypeStruct((b, d, s), BF16),
        grid=(b, d // hp, s // tq),
        in_specs=[
            pl.BlockSpec((None, hp, tq), lambda bi, p, i: (bi, p, i)),
            pl.BlockSpec((None, s, hp), lambda bi, p, i: (bi, 0, p)),
            pl.BlockSpec((None, s // tk, hp, tk), lambda bi, p, i: (bi, 0, p, 0)),
        ],
        out_specs=pl.BlockSpec((None, hp, tq), lambda bi, p, i: (bi, p, i)),
        compiler_params=_params(3),
        name="sb_attn",
    )(qt, k, vt)


def _oproj_kernel(ot_ref, x_ref, mod_ref, w_ref, o_ref, *, sub):
    _, _, gate = _mod_rows(mod_ref, sub)
    y = lax.dot_general(ot_ref[...], w_ref[...], (((0,), (0,)), ((), ())),
                        preferred_element_type=F32)
    o_ref[...] = x_ref[...] + (1.0 + gate) * y


def _oproj_call(ot, x, mod_l, w_o, sub):
    b, s, d = x.shape
    tm = 512
    return pl.pallas_call(
        functools.partial(_oproj_kernel, sub=sub),
        out_shape=jax.ShapeDtypeStruct((b, s, d), F32),
        grid=(b, s // tm),
        in_specs=[
            pl.BlockSpec((None, d, tm), lambda bi, i: (bi, 0, i)),
            pl.BlockSpec((None, tm, d), lambda bi, i: (bi, i, 0)),
            pl.BlockSpec((None, N_SUB * 3, d), lambda bi, i: (bi, 0, 0)),
            _resident((d, d), lambda bi, i: (0, 0)),
        ],
        out_specs=pl.BlockSpec((None, tm, d), lambda bi, i: (bi, i, 0)),
        compiler_params=_params(2),
        name="sb_oproj",
    )(ot, x, mod_l, w_o.astype(BF16))


def _lru_kernel(x_ref, mod_ref, g_ref, win_ref, cw_ref, cb_ref, wr_ref, br_ref, wi_ref, bi_ref,
                lam_ref, wout_ref, o_ref, xb_scr, a_scr, b_scr, h_scr, *, sub):
    ts = x_ref.shape[0]
    dr = cb_ref.shape[-1]
    n_taps = cw_ref.shape[0]
    pad = 8

    @pl.when(pl.program_id(1) == 0)
    def _():
        xb_scr[0:pad, :] = jnp.zeros((pad, dr), F32)
        h_scr[...] = jnp.zeros_like(h_scr)

    x = x_ref[...]
    shift, scale, gate = _mod_rows(mod_ref, sub)
    h = _norm_mod(x, g_ref[sub:sub + 1, :], shift, scale).astype(BF16)
    gate_br = jnp.dot(h, win_ref[:, 0:dr], preferred_element_type=F32)
    xb_scr[pad:pad + ts, :] = jnp.dot(h, win_ref[:, dr:2 * dr], preferred_element_type=F32)

    xc = cb_ref[...]
    for k in range(n_taps):
        off = pad - (n_taps - 1) + k
        xc = xc + xb_scr[off:off + ts, :] * cw_ref[k:k + 1, :]
    xb_scr[0:pad, :] = xb_scr[ts:ts + pad, :]

    bw = wr_ref.shape[-1]
    xcb = xc.astype(BF16)
    r_parts, i_parts = [], []
    for n in range(dr // bw):
        blk = xcb[:, n * bw:(n + 1) * bw]
        r_parts.append(jnp.dot(blk, wr_ref[n], preferred_element_type=F32))
        i_parts.append(jnp.dot(blk, wi_ref[n], preferred_element_type=F32))
    r = jax.nn.sigmoid(jnp.concatenate(r_parts, axis=1) + br_ref[...])
    ig = jax.nn.sigmoid(jnp.concatenate(i_parts, axis=1) + bi_ref[...])

    nl = -lam_ref[...]
    softplus_nl = jnp.maximum(nl, 0.0) + jnp.log(1.0 + jnp.exp(-jnp.abs(nl)))
    a = jnp.exp((-LRU_C) * r * softplus_nl)
    a_scr[...] = a
    b_scr[...] = jnp.sqrt(1.0 - a * a) * (ig * xc)

    unroll = 8

    def step(i, hprev):
        for j in range(unroll):
            t = i * unroll + j
            hprev = a_scr[pl.ds(t, 1), :] * hprev + b_scr[pl.ds(t, 1), :]
            b_scr[pl.ds(t, 1), :] = hprev
        return hprev

    h_scr[...] = lax.fori_loop(0, ts // unroll, step, h_scr[...])

    y = (jax.nn.gelu(gate_br) * b_scr[...]).astype(BF16)
    out = jnp.dot(y, wout_ref[...], preferred_element_type=F32)
    o_ref[...] = x + (1.0 + gate) * out


def _lru_call(x, mod_l, norm_g_l, w_in, conv_w, conv_b, w_r, b_r, w_i, b_i, lam, w_out, sub):
    b, s, d = x.shape
    dr = w_out.shape[0]
    nb, bw, _ = w_r.shape
    ts = 256
    assert s % ts == 0 and dr == nb * bw and bw % LANES == 0
    row = lambda v: v.reshape(1, dr)
    const2 = lambda bi, i: (0, 0)
    const3 = lambda bi, i: (0, 0, 0)
    return pl.pallas_call(
        functools.partial(_lru_kernel, sub=sub),
        out_shape=jax.ShapeDtypeStruct((b, s, d), F32),
        grid=(b, s // ts),
        in_specs=[
            pl.BlockSpec((None, ts, d), lambda bi, i: (bi, i, 0)),
            pl.BlockSpec((None, N_SUB * 3, d), lambda bi, i: (bi, 0, 0)),
            pl.BlockSpec((N_SUB, d), const2),
            _resident((d, 2 * dr), const2),
            pl.BlockSpec(conv_w.shape, const2),
            pl.BlockSpec((1, dr), const2),
            _resident((nb, bw, bw), const3),
            pl.BlockSpec((1, dr), const2),
            _resident((nb, bw, bw), const3),
            pl.BlockSpec((1, dr), const2),
            pl.BlockSpec((1, dr), const2),
            _resident((dr, d), const2),
        ],
        out_specs=pl.BlockSpec((None, ts, d), lambda bi, i: (bi, i, 0)),
        scratch_shapes=[
            pltpu.VMEM((ts + 8, dr), F32),
            pltpu.VMEM((ts, dr), F32),
            pltpu.VMEM((ts, dr), F32),
            pltpu.VMEM((1, dr), F32),
        ],
        compiler_params=_params(2),
        name="rglru",
    )(x, mod_l, norm_g_l, w_in.astype(BF16), conv_w, row(conv_b), w_r.astype(BF16), row(b_r),
      w_i.astype(BF16), row(b_i), row(lam), w_out.astype(BF16))


def kernel(x, c, mod_w, mod_b, norm_g, ffn_w_gu, ffn_w_down, sb_w_qkv, sb_w_o, lru_w_in, lru_conv_w,
           lru_conv_b, lru_w_r, lru_b_r, lru_w_i, lru_b_i, lru_lambda, lru_w_out, final_norm_g):
    depth = mod_w.shape[0]
    b, s, d = x.shape
    mod = _mod_call(c, mod_w, mod_b).reshape(depth, b, N_SUB * 3, d)
    for layer in range(depth):
        mod_l, ng = mod[layer], norm_g[layer]
        x = _ffn_call(x, mod_l, ng, ffn_w_gu[layer, 0], ffn_w_down[layer, 0], sub=0)
        j = layer // 2
        if layer % 2 == 0:
            qt, k, vt = _qkv_call(x, mod_l, ng, sb_w_qkv[j], sub=1, tk=MXU_TILE)
            ot = _attn_call(qt, k, vt)
            x = _oproj_call(ot, x, mod_l, sb_w_o[j], sub=1)
        else:
            x = _lru_call(x, mod_l, ng, lru_w_in[j], lru_conv_w[j], lru_conv_b[j], lru_w_r[j],
                          lru_b_r[j], lru_w_i[j], lru_b_i[j], lru_lambda[j], lru_w_out[j], sub=1)
        last = layer == depth - 1
        x = _ffn_call(x, mod_l, ng, ffn_w_gu[layer, 1], ffn_w_down[layer, 1], sub=2,
                      final_g=final_norm_g if last else None)
    return x
```

```python
import functools

import jax
import jax.numpy as jnp
import numpy as np
from jax import lax
from jax.experimental import pallas as pl
from jax.experimental.pallas import tpu as pltpu

SB_HEADS = 16
LRU_C = 8.0
MACARON_W = 0.5
EPS = 1e-6
N_SUB = 3

LANES = 128
MXU_TILE = 256
VMEM_LIMIT_BYTES = 56 * 2 ** 20

ATTN_HEADS_PER_STEP = 8
NEG_BIG = -1e30
LOG2_E = 1.4426950408889634
TINY = 1e-30

F32 = jnp.float32
BF16 = jnp.bfloat16


def _params(n_grid_axes):
    return pltpu.CompilerParams(
        dimension_semantics=("arbitrary",) * n_grid_axes,
        vmem_limit_bytes=VMEM_LIMIT_BYTES,
    )


def _resident(block_shape, index_map):
    return pl.BlockSpec(block_shape, index_map, pipeline_mode=pl.Buffered(1))


def _norm_mod(x, g, shift, scale):
    inv = lax.rsqrt(jnp.mean(x * x, axis=-1, keepdims=True) + EPS)
    return (x * inv) * (g * (1.0 + scale)) + shift


def _mod_rows(mod_ref, sub):
    r = N_SUB * sub
    return mod_ref[r:r + 1, :], mod_ref[r + 1:r + 2, :], mod_ref[r + 2:r + 3, :]


def _mod_kernel(c_ref, w_ref, b_ref, o_ref):
    c = c_ref[...]
    ca = c * jax.nn.sigmoid(c)
    ca2 = jnp.concatenate([ca, ca], axis=0).astype(BF16)
    y = jnp.dot(ca2, w_ref[...].astype(BF16), preferred_element_type=F32)
    o_ref[...] = y[:c.shape[0]] + b_ref[...]


def _mod_call(c, mod_w, mod_b):
    depth, d, n = mod_w.shape
    b = c.shape[0]
    tn = 1024
    assert n % tn == 0
    return pl.pallas_call(
        _mod_kernel,
        out_shape=jax.ShapeDtypeStruct((depth, b, n), F32),
        grid=(depth, n // tn),
        in_specs=[
            pl.BlockSpec((b, d), lambda l, j: (0, 0)),
            pl.BlockSpec((None, d, tn), lambda l, j: (l, 0, j)),
            pl.BlockSpec((None, 1, tn), lambda l, j: (l, 0, j)),
        ],
        out_specs=pl.BlockSpec((None, b, tn), lambda l, j: (l, 0, j)),
        compiler_params=_params(2),
        name="mod",
    )(c, mod_w, mod_b.reshape(depth, 1, n))


def _ffn_kernel(x_ref, mod_ref, g_ref, wg_ref, wu_ref, wd_ref, *rest, sub, final):
    if final:
        fg_ref, o_ref = rest
    else:
        (o_ref,) = rest
    x = x_ref[...]
    shift, scale, gate = _mod_rows(mod_ref, sub)
    h = _norm_mod(x, g_ref[sub:sub + 1, :], shift, scale).astype(BF16)
    acts = []
    for c in range(wg_ref.shape[0]):
        g = jnp.dot(h, wg_ref[c], preferred_element_type=F32)
        u = jnp.dot(h, wu_ref[c], preferred_element_type=F32)
        acts.append((g * jax.nn.sigmoid(g) * u).astype(BF16))
    y = jnp.dot(jnp.concatenate(acts, axis=1), wd_ref[...], preferred_element_type=F32)
    out = x + (MACARON_W * (1.0 + gate)) * y
    if final:
        inv = lax.rsqrt(jnp.mean(out * out, axis=-1, keepdims=True) + EPS)
        out = (out * inv) * fg_ref[...]
    o_ref[...] = out


def _ffn_call(x, mod_l, norm_g_l, w_gu, w_down, sub, final_g=None):
    b, s, d = x.shape
    d_ff = w_down.shape[0]
    tm = 512
    tf = MXU_TILE
    assert s % tm == 0 and d_ff % tf == 0
    nc = d_ff // tf
    wg = w_gu[:, :d_ff].reshape(d, nc, tf).transpose(1, 0, 2).astype(BF16)
    wu = w_gu[:, d_ff:].reshape(d, nc, tf).transpose(1, 0, 2).astype(BF16)
    wd = w_down.astype(BF16)
    final = final_g is not None
    in_specs = [
        pl.BlockSpec((None, tm, d), lambda bi, i: (bi, i, 0)),
        pl.BlockSpec((None, N_SUB * 3, d), lambda bi, i: (bi, 0, 0)),
        pl.BlockSpec((N_SUB, d), lambda bi, i: (0, 0)),
        _resident((nc, d, tf), lambda bi, i: (0, 0, 0)),
        _resident((nc, d, tf), lambda bi, i: (0, 0, 0)),
        _resident((d_ff, d), lambda bi, i: (0, 0)),
    ]
    args = [x, mod_l, norm_g_l, wg, wu, wd]
    if final:
        in_specs.append(pl.BlockSpec((1, d), lambda bi, i: (0, 0)))
        args.append(final_g.reshape(1, d))
    return pl.pallas_call(
        functools.partial(_ffn_kernel, sub=sub, final=final),
        out_shape=jax.ShapeDtypeStruct((b, s, d), F32),
        grid=(b, s // tm),
        in_specs=in_specs,
        out_specs=pl.BlockSpec((None, tm, d), lambda bi, i: (bi, i, 0)),
        compiler_params=_params(2),
        name="ffn_final" if final else "ffn",
    )(*args)


def _qkv_kernel(x_ref, mod_ref, g_ref, w_ref, qt_ref, k_ref, vt_ref, *, sub, scale_q):
    d = x_ref.shape[-1]
    shift, scale, _ = _mod_rows(mod_ref, sub)
    h = _norm_mod(x_ref[...], g_ref[sub:sub + 1, :], shift, scale).astype(BF16)
    q = jnp.dot(h, w_ref[:, 0:d], preferred_element_type=F32) * scale_q
    k_ref[...] = jnp.dot(h, w_ref[:, d:2 * d], preferred_element_type=F32).astype(BF16)
    v = jnp.dot(h, w_ref[:, 2 * d:3 * d], preferred_element_type=F32)
    tt = vt_ref.shape[-1]
    for j in range(vt_ref.shape[0]):
        qt_ref[j] = q[j * tt:(j + 1) * tt, :].T.astype(BF16)
        vt_ref[j] = v[j * tt:(j + 1) * tt, :].T.astype(BF16)


def _qkv_call(x, mod_l, norm_g_l, w_qkv, sub, tt):
    b, s, d = x.shape
    tm = 512
    assert s % tm == 0 and tm % tt == 0
    dh = d // SB_HEADS
    tiled_t = jax.ShapeDtypeStruct((b, s // tt, d, tt), BF16)
    tiled_spec = pl.BlockSpec((None, tm // tt, d, tt), lambda bi, i: (bi, i, 0, 0))
    return pl.pallas_call(
        functools.partial(_qkv_kernel, sub=sub, scale_q=dh ** -0.5 * LOG2_E),
        out_shape=(tiled_t,
                   jax.ShapeDtypeStruct((b, s, d), BF16),
                   tiled_t),
        grid=(b, s // tm),
        in_specs=[
            pl.BlockSpec((None, tm, d), lambda bi, i: (bi, i, 0)),
            pl.BlockSpec((None, N_SUB * 3, d), lambda bi, i: (bi, 0, 0)),
            pl.BlockSpec((N_SUB, d), lambda bi, i: (0, 0)),
            _resident((d, 3 * d), lambda bi, i: (0, 0)),
        ],
        out_specs=(tiled_spec, pl.BlockSpec((None, tm, d), lambda bi, i: (bi, i, 0)), tiled_spec),
        compiler_params=_params(2),
        name="qkv",
    )(x, mod_l, norm_g_l, w_qkv.astype(BF16))


def _attn_tables(n_qblocks):
    qb, kb, dg = [0, 0], [0, 0], [1, 1]
    for i in range(n_qblocks):
        for t in range(i + 1):
            qb.append(i)
            kb.append(i - t)
            dg.append(1 if t == 0 else 0)
    n_back = 2 + len(qb) % 2
    qb, kb, dg = qb + [0] * n_back, kb + [0] * n_back, dg + [1] * n_back
    return tuple(jnp.asarray(np.asarray(v, np.int32)) for v in (qb, kb, dg))


def _attn_kernel(qb_tab, kb_tab, dg_tab, qt_ref, k_ref, vt_ref, ot_ref,
                 z_scr, lb_scr, later_scr, acc_scr, tail_scr, tail_at_scr, cap_scr, *, dh):
    tq = qt_ref.shape[-1]
    tk = vt_ref.shape[-1]
    hp = 2 * dh
    n_heads = qt_ref.shape[1] // dh
    n_iter = qb_tab.shape[0] - 2

    kr = lax.broadcasted_iota(jnp.int32, (tk, tq), 0)
    qc = lax.broadcasted_iota(jnp.int32, (tk, tq), 1)
    cap_scr[0] = jnp.full((tk, tq), jnp.inf, F32)
    cap_scr[1] = jnp.where(kr < qc, jnp.inf, NEG_BIG).astype(F32)
    z_scr[1] = jnp.full(z_scr.shape[1:], NEG_BIG, F32)
    lb_scr[...] = jnp.full(lb_scr.shape, NEG_BIG, F32)
    later_scr[...] = jnp.zeros(later_scr.shape, F32)
    acc_scr[...] = jnp.zeros(acc_scr.shape, F32)
    tail_scr[...] = jnp.zeros(tail_scr.shape, F32)
    tail_at_scr[...] = jnp.zeros(tail_at_scr.shape, F32)

    jr = lax.broadcasted_iota(jnp.int32, (tk, tk), 0)
    jc = lax.broadcasted_iota(jnp.int32, (tk, tk), 1)
    later_mat = (jc > jr).astype(BF16)
    frow = lax.broadcasted_iota(jnp.int32, (hp, tq), 0)

    def stages(m, slot):
        qb_s, kb_s, dg_s = qb_tab[m + 2], kb_tab[m + 2], dg_tab[m + 2]
        dg_m = dg_tab[m + 1]
        qb_o, kb_o = qb_tab[m], kb_tab[m]
        keep_tail = jnp.where(dg_m == 1, 0.0, 1.0)
        cap = cap_scr[dg_s]
        k_all = k_ref[pl.ds(pl.multiple_of(kb_s * tk, tk), tk), :]
        qt_all = qt_ref[qb_s]
        vt_all = vt_ref[kb_o]
        for hd in range(n_heads):
            p, sub = divmod(hd, 2)
            pair = slice(p * hp, (p + 1) * hp)
            qt = qt_all[pair, :]
            qt = jnp.where((frow >= sub * dh) & (frow < (sub + 1) * dh), qt, jnp.zeros_like(qt))
            z = jnp.dot(k_all[:, pair], qt, preferred_element_type=F32)
            z_scr[slot, hd] = jnp.minimum(z, cap)
            w = jnp.exp2(lb_scr[hd] + later_scr[hd]).astype(BF16)
            pv = jnp.dot(vt_all[pair, :], w, preferred_element_type=F32)
            rows = slice(hd * dh, (hd + 1) * dh)
            acc_scr[qb_o, rows, :] += pv[sub * dh:(sub + 1) * dh, :] * jnp.exp2(tail_at_scr[hd])
            zz = z_scr[1 - slot, hd]
            log_beta = jnp.minimum(zz, 0.0) - jnp.log(1.0 + jnp.exp2(-jnp.abs(zz))) * LOG2_E
            log_keep = log_beta - zz
            lb_scr[hd] = log_beta
            later = jnp.dot(later_mat, log_keep.astype(BF16), preferred_element_type=F32)
            later_scr[hd] = later
            tail_at = tail_scr[hd] * keep_tail
            tail_at_scr[hd] = tail_at
            tail_scr[hd] = tail_at + later[0:1, :] + log_keep[0:1, :]

    def body(i, carry):
        stages(2 * i, 0)
        stages(2 * i + 1, 1)
        return carry

    lax.fori_loop(0, n_iter // 2, body, 0)
    ot_ref[...] = acc_scr[...].astype(BF16)


def _attn_call(qt, k, vt):
    b, nq, d, tq = qt.shape
    tk = vt.shape[-1]
    s = k.shape[1]
    dh = d // SB_HEADS
    assert 2 * dh == LANES and tq == tk and nq * tq == s
    hg = ATTN_HEADS_PER_STEP * dh
    assert ATTN_HEADS_PER_STEP % 2 == 0 and d % hg == 0
    tiled_spec = pl.BlockSpec((None, nq, hg, tq), lambda bi, g, *_: (bi, 0, g, 0))
    grid_spec = pltpu.PrefetchScalarGridSpec(
        num_scalar_prefetch=3,
        grid=(b, d // hg),
        in_specs=[tiled_spec, pl.BlockSpec((None, s, hg), lambda bi, g, *_: (bi, 0, g)), tiled_spec],
        out_specs=tiled_spec,
        scratch_shapes=[
            pltpu.VMEM((2, ATTN_HEADS_PER_STEP, tk, tq), F32),
            pltpu.VMEM((ATTN_HEADS_PER_STEP, tk, tq), F32),
            pltpu.VMEM((ATTN_HEADS_PER_STEP, tk, tq), F32),
            pltpu.VMEM((nq, hg, tq), F32),
            pltpu.VMEM((ATTN_HEADS_PER_STEP, 1, tq), F32),
            pltpu.VMEM((ATTN_HEADS_PER_STEP, 1, tq), F32),
            pltpu.VMEM((2, tk, tq), F32),
        ],
    )
    return pl.pallas_call(
        functools.partial(_attn_kernel, dh=dh),
        out_shape=jax.ShapeDtypeStruct(qt.shape, BF16),
        grid_spec=grid_spec,
        compiler_params=_params(2),
        name="sb_attn",
    )(*_attn_tables(nq), qt, k, vt)


def _oproj_kernel(ot_ref, x_ref, mod_ref, w_ref, o_ref, *, sub):
    _, _, gate = _mod_rows(mod_ref, sub)
    tt = ot_ref.shape[-1]
    for j in range(ot_ref.shape[0]):
        y = lax.dot_general(ot_ref[j], w_ref[...], (((0,), (0,)), ((), ())),
                            preferred_element_type=F32)
        rows = slice(j * tt, (j + 1) * tt)
        o_ref[rows, :] = x_ref[rows, :] + (1.0 + gate) * y


def _oproj_call(ot, x, mod_l, w_o, sub):
    b, s, d = x.shape
    tt = ot.shape[-1]
    tm = 512
    assert s % tm == 0 and tm % tt == 0
    return pl.pallas_call(
        functools.partial(_oproj_kernel, sub=sub),
        out_shape=jax.ShapeDtypeStruct((b, s, d), F32),
        grid=(b, s // tm),
        in_specs=[
            pl.BlockSpec((None, tm // tt, d, tt), lambda bi, i: (bi, i, 0, 0)),
            pl.BlockSpec((None, tm, d), lambda bi, i: (bi, i, 0)),
            pl.BlockSpec((None, N_SUB * 3, d), lambda bi, i: (bi, 0, 0)),
            _resident((d, d), lambda bi, i: (0, 0)),
        ],
        out_specs=pl.BlockSpec((None, tm, d), lambda bi, i: (bi, i, 0)),
        compiler_params=_params(2),
        name="sb_oproj",
    )(ot, x, mod_l, w_o.astype(BF16))


def _lru_kernel(x_ref, mod_ref, g_ref, win_ref, cw_ref, cb_ref, wr_ref, br_ref, wi_ref, bi_ref,
                lam_ref, wout_ref, o_ref, xb_scr, a_scr, b_scr, ga_scr, gb_scr, hin_scr, h_scr, *, sub):
    ts = x_ref.shape[0]
    dr = cb_ref.shape[-1]
    n_taps = cw_ref.shape[0]
    pad = 8

    @pl.when(pl.program_id(1) == 0)
    def _():
        xb_scr[0:pad, :] = jnp.zeros((pad, dr), F32)
        h_scr[...] = jnp.zeros_like(h_scr)

    x = x_ref[...]
    shift, scale, gate = _mod_rows(mod_ref, sub)
    h = _norm_mod(x, g_ref[sub:sub + 1, :], shift, scale).astype(BF16)
    gate_br = jnp.dot(h, win_ref[:, 0:dr], preferred_element_type=F32)
    xb_scr[pad:pad + ts, :] = jnp.dot(h, win_ref[:, dr:2 * dr], preferred_element_type=F32)

    xc = cb_ref[...]
    for k in range(n_taps):
        off = pad - (n_taps - 1) + k
        xc = xc + xb_scr[off:off + ts, :] * cw_ref[k:k + 1, :]
    xb_scr[0:pad, :] = xb_scr[ts:ts + pad, :]

    bw = wr_ref.shape[-1]
    xcb = xc.astype(BF16)
    r_parts, i_parts = [], []
    for n in range(dr // bw):
        blk = xcb[:, n * bw:(n + 1) * bw]
        r_parts.append(jnp.dot(blk, wr_ref[n], preferred_element_type=F32))
        i_parts.append(jnp.dot(blk, wi_ref[n], preferred_element_type=F32))
    r = jax.nn.sigmoid(jnp.concatenate(r_parts, axis=1) + br_ref[...])
    ig = jax.nn.sigmoid(jnp.concatenate(i_parts, axis=1) + bi_ref[...])

    nl = -lam_ref[...]
    softplus_nl = jnp.maximum(nl, 0.0) + jnp.log(1.0 + jnp.exp(-jnp.abs(nl)))
    a = jnp.exp((-LRU_C) * r * softplus_nl)
    v = 1.0 - a * a
    b = (v * lax.rsqrt(jnp.maximum(v, TINY))) * (ig * xc)

    n_slabs, grp = dr // LANES, 8
    n_grp = ts // grp
    for n in range(n_slabs):
        a_scr[n] = a[:, n * LANES:(n + 1) * LANES]
        b_scr[n] = b[:, n * LANES:(n + 1) * LANES]
    cum = []
    for n in range(n_slabs):
        ca = cb = None
        per_step = []
        for s in range(grp):
            a_s = a_scr[n, pl.ds(s, n_grp, stride=grp), :]
            b_s = b_scr[n, pl.ds(s, n_grp, stride=grp), :]
            ca, cb = (a_s, b_s) if s == 0 else (a_s * ca, a_s * cb + b_s)
            per_step.append((ca, cb))
        ga_scr[n], gb_scr[n] = ca, cb
        cum.append(per_step)

    def group_step(g, hs):
        out = []
        for n in range(n_slabs):
            hin_scr[n, pl.ds(g, 1), :] = hs[n]
            out.append(ga_scr[n, pl.ds(g, 1), :] * hs[n] + gb_scr[n, pl.ds(g, 1), :])
        return tuple(out)

    h0 = tuple(h_scr[:, n * LANES:(n + 1) * LANES] for n in range(n_slabs))
    h_end = lax.fori_loop(0, n_grp, group_step, h0)
    for n in range(n_slabs):
        h_scr[:, n * LANES:(n + 1) * LANES] = h_end[n]
        h_in = hin_scr[n]
        for s in range(grp):
            ca, cb = cum[n][s]
            b_scr[n, pl.ds(s, n_grp, stride=grp), :] = ca * h_in + cb
    hs = jnp.concatenate([b_scr[n] for n in range(n_slabs)], axis=1)

    y = (jax.nn.gelu(gate_br) * hs).astype(BF16)
    out = jnp.dot(y, wout_ref[...], preferred_element_type=F32)
    o_ref[...] = x + (1.0 + gate) * out


def _lru_call(x, mod_l, norm_g_l, w_in, conv_w, conv_b, w_r, b_r, w_i, b_i, lam, w_out, sub):
    b, s, d = x.shape
    dr = w_out.shape[0]
    nb, bw, _ = w_r.shape
    ts = 256
    assert s % ts == 0 and dr == nb * bw and bw % LANES == 0
    row = lambda v: v.reshape(1, dr)
    const2 = lambda bi, i: (0, 0)
    const3 = lambda bi, i: (0, 0, 0)
    return pl.pallas_call(
        functools.partial(_lru_kernel, sub=sub),
        out_shape=jax.ShapeDtypeStruct((b, s, d), F32),
        grid=(b, s // ts),
        in_specs=[
            pl.BlockSpec((None, ts, d), lambda bi, i: (bi, i, 0)),
            pl.BlockSpec((None, N_SUB * 3, d), lambda bi, i: (bi, 0, 0)),
            pl.BlockSpec((N_SUB, d), const2),
            _resident((d, 2 * dr), const2),
            pl.BlockSpec(conv_w.shape, const2),
            pl.BlockSpec((1, dr), const2),
            _resident((nb, bw, bw), const3),
            pl.BlockSpec((1, dr), const2),
            _resident((nb, bw, bw), const3),
            pl.BlockSpec((1, dr), const2),
            pl.BlockSpec((1, dr), const2),
            _resident((dr, d), const2),
        ],
        out_specs=pl.BlockSpec((None, ts, d), lambda bi, i: (bi, i, 0)),
        scratch_shapes=[
            pltpu.VMEM((ts + 8, dr), F32),
            pltpu.VMEM((dr // LANES, ts, LANES), F32),
            pltpu.VMEM((dr // LANES, ts, LANES), F32),
            pltpu.VMEM((dr // LANES, ts // 8, LANES), F32),
            pltpu.VMEM((dr // LANES, ts // 8, LANES), F32),
            pltpu.VMEM((dr // LANES, ts // 8, LANES), F32),
            pltpu.VMEM((1, dr), F32),
        ],
        compiler_params=_params(2),
        name="rglru",
    )(x, mod_l, norm_g_l, w_in.astype(BF16), conv_w, row(conv_b), w_r.astype(BF16), row(b_r),
      w_i.astype(BF16), row(b_i), row(lam), w_out.astype(BF16))


def kernel(x, c, mod_w, mod_b, norm_g, ffn_w_gu, ffn_w_down, sb_w_qkv, sb_w_o, lru_w_in, lru_conv_w,
           lru_conv_b, lru_w_r, lru_b_r, lru_w_i, lru_b_i, lru_lambda, lru_w_out, final_norm_g):
    depth = mod_w.shape[0]
    b, s, d = x.shape
    mod = _mod_call(c, mod_w, mod_b).reshape(depth, b, N_SUB * 3, d)
    for layer in range(depth):
        mod_l, ng = mod[layer], norm_g[layer]
        x = _ffn_call(x, mod_l, ng, ffn_w_gu[layer, 0], ffn_w_down[layer, 0], sub=0)
        j = layer // 2
        if layer % 2 == 0:
            qt, k, vt = _qkv_call(x, mod_l, ng, sb_w_qkv[j], sub=1, tt=MXU_TILE)
            ot = _attn_call(qt, k, vt)
            x = _oproj_call(ot, x, mod_l, sb_w_o[j], sub=1)
        else:
            x = _lru_call(x, mod_l, ng, lru_w_in[j], lru_conv_w[j], lru_conv_b[j], lru_w_r[j],
                          lru_b_r[j], lru_w_i[j], lru_b_i[j], lru_lambda[j], lru_w_out[j], sub=1)
        last = layer == depth - 1
        x = _ffn_call(x, mod_l, ng, ffn_w_gu[layer, 1], ffn_w_down[layer, 1], sub=2,
                      final_g=final_norm_g if last else None)
    return x
```

```python
import functools

import jax
import jax.numpy as jnp
import numpy as np
from jax import lax
from jax.experimental import pallas as pl
from jax.experimental.pallas import tpu as pltpu

SB_HEADS = 16
LRU_C = 8.0
MACARON_W = 0.5
EPS = 1e-6
N_SUB = 3

LANES = 128
MXU_TILE = 256
VMEM_LIMIT_BYTES = 56 * 2 ** 20

ATTN_HEADS_PER_STEP = 8
NEG_BIG = -1e30
LOG2_E = 1.4426950408889634
TINY = 1e-30

F32 = jnp.float32
BF16 = jnp.bfloat16


def _params(n_grid_axes):
    return pltpu.CompilerParams(
        dimension_semantics=("arbitrary",) * n_grid_axes,
        vmem_limit_bytes=VMEM_LIMIT_BYTES,
    )


def _resident(block_shape, index_map):
    return pl.BlockSpec(block_shape, index_map, pipeline_mode=pl.Buffered(1))


def _norm_mod(x, g, shift, scale):
    inv = lax.rsqrt(jnp.mean(x * x, axis=-1, keepdims=True) + EPS)
    return (x * inv) * (g * (1.0 + scale)) + shift


def _mod_rows(mod_ref, sub):
    r = N_SUB * sub
    return mod_ref[r:r + 1, :], mod_ref[r + 1:r + 2, :], mod_ref[r + 2:r + 3, :]


def _mod_kernel(c_ref, w_ref, b_ref, o_ref):
    c = c_ref[...]
    ca = c * jax.nn.sigmoid(c)
    ca2 = jnp.concatenate([ca, ca], axis=0).astype(BF16)
    y = jnp.dot(ca2, w_ref[...].astype(BF16), preferred_element_type=F32)
    o_ref[...] = y[:c.shape[0]] + b_ref[...]


def _mod_call(c, mod_w, mod_b):
    depth, d, n = mod_w.shape
    b = c.shape[0]
    tn = 1024
    assert n % tn == 0
    return pl.pallas_call(
        _mod_kernel,
        out_shape=jax.ShapeDtypeStruct((depth, b, n), F32),
        grid=(depth, n // tn),
        in_specs=[
            pl.BlockSpec((b, d), lambda l, j: (0, 0)),
            pl.BlockSpec((None, d, tn), lambda l, j: (l, 0, j)),
            pl.BlockSpec((None, 1, tn), lambda l, j: (l, 0, j)),
        ],
        out_specs=pl.BlockSpec((None, b, tn), lambda l, j: (l, 0, j)),
        compiler_params=_params(2),
        name="mod",
    )(c, mod_w, mod_b.reshape(depth, 1, n))


def _ffn_kernel(x_ref, mod_ref, g_ref, wgu_ref, wd_ref, *rest, sub, final):
    if final:
        fg_ref, o_ref = rest
    else:
        (o_ref,) = rest
    x = x_ref[...]
    d_ff = wd_ref.shape[0]
    tf = MXU_TILE
    shift, scale, gate = _mod_rows(mod_ref, sub)
    h = _norm_mod(x, g_ref[sub:sub + 1, :], shift, scale).astype(BF16)
    acts = []
    for c in range(d_ff // tf):
        g = jnp.dot(h, wgu_ref[:, c * tf:(c + 1) * tf], preferred_element_type=F32)
        u = jnp.dot(h, wgu_ref[:, d_ff + c * tf:d_ff + (c + 1) * tf], preferred_element_type=F32)
        acts.append((g * jax.nn.sigmoid(g) * u).astype(BF16))
    y = jnp.dot(jnp.concatenate(acts, axis=1), wd_ref[...], preferred_element_type=F32)
    out = x + (MACARON_W * (1.0 + gate)) * y
    if final:
        inv = lax.rsqrt(jnp.mean(out * out, axis=-1, keepdims=True) + EPS)
        out = (out * inv) * fg_ref[...]
    o_ref[...] = out


def _ffn_call(x, mod_l, norm_g_l, w_gu_all, w_down_all, layer, half, sub, final_g=None):
    b, s, d = x.shape
    d_ff = w_down_all.shape[-2]
    tm = 512
    assert s % tm == 0 and d_ff % MXU_TILE == 0
    final = final_g is not None
    in_specs = [
        pl.BlockSpec((None, tm, d), lambda bi, i: (bi, i, 0)),
        pl.BlockSpec((None, N_SUB * 3, d), lambda bi, i: (bi, 0, 0)),
        pl.BlockSpec((N_SUB, d), lambda bi, i: (0, 0)),
        _resident((None, None, d, 2 * d_ff), lambda bi, i: (layer, half, 0, 0)),
        _resident((None, None, d_ff, d), lambda bi, i: (layer, half, 0, 0)),
    ]
    args = [x, mod_l, norm_g_l, w_gu_all, w_down_all]
    if final:
        in_specs.append(pl.BlockSpec((1, d), lambda bi, i: (0, 0)))
        args.append(final_g.reshape(1, d))
    return pl.pallas_call(
        functools.partial(_ffn_kernel, sub=sub, final=final),
        out_shape=jax.ShapeDtypeStruct((b, s, d), F32),
        grid=(b, s // tm),
        in_specs=in_specs,
        out_specs=pl.BlockSpec((None, tm, d), lambda bi, i: (bi, i, 0)),
        compiler_params=_params(2),
        name="ffn_final" if final else "ffn",
    )(*args)


def _qkv_kernel(x_ref, mod_ref, g_ref, w_ref, qt_ref, k_ref, vt_ref, *, sub, scale_q):
    d = x_ref.shape[-1]
    shift, scale, _ = _mod_rows(mod_ref, sub)
    h = _norm_mod(x_ref[...], g_ref[sub:sub + 1, :], shift, scale).astype(BF16)
    q = jnp.dot(h, w_ref[:, 0:d], preferred_element_type=F32) * scale_q
    k_ref[...] = jnp.dot(h, w_ref[:, d:2 * d], preferred_element_type=F32).astype(BF16)
    v = jnp.dot(h, w_ref[:, 2 * d:3 * d], preferred_element_type=F32)
    tt = vt_ref.shape[-1]
    for j in range(vt_ref.shape[0]):
        qt_ref[j] = q[j * tt:(j + 1) * tt, :].T.astype(BF16)
        vt_ref[j] = v[j * tt:(j + 1) * tt, :].T.astype(BF16)


def _qkv_call(x, mod_l, norm_g_l, w_qkv, sub, tt):
    b, s, d = x.shape
    tm = 512
    assert s % tm == 0 and tm % tt == 0
    dh = d // SB_HEADS
    tiled_t = jax.ShapeDtypeStruct((b, s // tt, d, tt), BF16)
    tiled_spec = pl.BlockSpec((None, tm // tt, d, tt), lambda bi, i: (bi, i, 0, 0))
    return pl.pallas_call(
        functools.partial(_qkv_kernel, sub=sub, scale_q=dh ** -0.5 * LOG2_E),
        out_shape=(tiled_t,
                   jax.ShapeDtypeStruct((b, s, d), BF16),
                   tiled_t),
        grid=(b, s // tm),
        in_specs=[
            pl.BlockSpec((None, tm, d), lambda bi, i: (bi, i, 0)),
            pl.BlockSpec((None, N_SUB * 3, d), lambda bi, i: (bi, 0, 0)),
            pl.BlockSpec((N_SUB, d), lambda bi, i: (0, 0)),
            _resident((d, 3 * d), lambda bi, i: (0, 0)),
        ],
        out_specs=(tiled_spec, pl.BlockSpec((None, tm, d), lambda bi, i: (bi, i, 0)), tiled_spec),
        compiler_params=_params(2),
        name="qkv",
    )(x, mod_l, norm_g_l, w_qkv.astype(BF16))


def _attn_tables(n_qblocks):
    qb, kb, dg = [0, 0], [0, 0], [1, 1]
    for i in range(n_qblocks):
        for t in range(i + 1):
            qb.append(i)
            kb.append(i - t)
            dg.append(1 if t == 0 else 0)
    n_back = 2 + len(qb) % 2
    qb, kb, dg = qb + [0] * n_back, kb + [0] * n_back, dg + [1] * n_back
    return tuple(jnp.asarray(np.asarray(v, np.int32)) for v in (qb, kb, dg))


def _attn_kernel(qb_tab, kb_tab, dg_tab, qt_ref, k_ref, vt_ref, ot_ref,
                 z_scr, lb_scr, later_scr, acc_scr, tail_scr, tail_at_scr, cap_scr, *, dh):
    tq = qt_ref.shape[-1]
    tk = vt_ref.shape[-1]
    hp = 2 * dh
    n_heads = qt_ref.shape[1] // dh
    n_iter = qb_tab.shape[0] - 2

    kr = lax.broadcasted_iota(jnp.int32, (tk, tq), 0)
    qc = lax.broadcasted_iota(jnp.int32, (tk, tq), 1)
    cap_scr[0] = jnp.full((tk, tq), jnp.inf, F32)
    cap_scr[1] = jnp.where(kr < qc, jnp.inf, NEG_BIG).astype(F32)
    z_scr[1] = jnp.full(z_scr.shape[1:], NEG_BIG, F32)
    lb_scr[...] = jnp.full(lb_scr.shape, NEG_BIG, F32)
    later_scr[...] = jnp.zeros(later_scr.shape, F32)
    acc_scr[...] = jnp.zeros(acc_scr.shape, F32)
    tail_scr[...] = jnp.zeros(tail_scr.shape, F32)
    tail_at_scr[...] = jnp.zeros(tail_at_scr.shape, F32)

    jr = lax.broadcasted_iota(jnp.int32, (tk, tk), 0)
    jc = lax.broadcasted_iota(jnp.int32, (tk, tk), 1)
    later_mat = (jc > jr).astype(BF16)
    frow = lax.broadcasted_iota(jnp.int32, (hp, tq), 0)

    def stages(m, slot):
        qb_s, kb_s, dg_s = qb_tab[m + 2], kb_tab[m + 2], dg_tab[m + 2]
        dg_m = dg_tab[m + 1]
        qb_o, kb_o = qb_tab[m], kb_tab[m]
        keep_tail = jnp.where(dg_m == 1, 0.0, 1.0)
        cap = cap_scr[dg_s]
        k_all = k_ref[pl.ds(pl.multiple_of(kb_s * tk, tk), tk), :]
        qt_all = qt_ref[qb_s]
        vt_all = vt_ref[kb_o]
        for hd in range(n_heads):
            p, sub = divmod(hd, 2)
            pair = slice(p * hp, (p + 1) * hp)
            w = jnp.exp2(lb_scr[hd] + later_scr[hd]).astype(BF16)
            pv = jnp.dot(vt_all[pair, :], w, preferred_element_type=F32)
            rows = slice(hd * dh, (hd + 1) * dh)
            acc_scr[qb_o, rows, :] += pv[sub * dh:(sub + 1) * dh, :] * jnp.exp2(tail_at_scr[hd])
            zz = z_scr[1 - slot, hd]
            log_beta = jnp.minimum(zz, 0.0) - jnp.log(1.0 + jnp.exp2(-jnp.abs(zz))) * LOG2_E
            log_keep = log_beta - zz
            lb_scr[hd] = log_beta
            later = jnp.dot(later_mat, log_keep.astype(BF16), preferred_element_type=F32)
            later_scr[hd] = later
            tail_at = tail_scr[hd] * keep_tail
            tail_at_scr[hd] = tail_at
            tail_scr[hd] = tail_at + later[0:1, :] + log_keep[0:1, :]
            qt = qt_all[pair, :]
            qt = jnp.where((frow >= sub * dh) & (frow < (sub + 1) * dh), qt, jnp.zeros_like(qt))
            z = jnp.dot(k_all[:, pair], qt, preferred_element_type=F32)
            z_scr[slot, hd] = jnp.minimum(z, cap)

    def body(i, carry):
        stages(2 * i, 0)
        stages(2 * i + 1, 1)
        return carry

    lax.fori_loop(0, n_iter // 2, body, 0)
    ot_ref[...] = acc_scr[...].astype(BF16)


def _attn_call(qt, k, vt):
    b, nq, d, tq = qt.shape
    tk = vt.shape[-1]
    s = k.shape[1]
    dh = d // SB_HEADS
    assert 2 * dh == LANES and tq == tk and nq * tq == s
    hg = ATTN_HEADS_PER_STEP * dh
    assert ATTN_HEADS_PER_STEP % 2 == 0 and d % hg == 0
    tiled_spec = pl.BlockSpec((None, nq, hg, tq), lambda bi, g, *_: (bi, 0, g, 0))
    grid_spec = pltpu.PrefetchScalarGridSpec(
        num_scalar_prefetch=3,
        grid=(b, d // hg),
        in_specs=[tiled_spec, pl.BlockSpec((None, s, hg), lambda bi, g, *_: (bi, 0, g)), tiled_spec],
        out_specs=tiled_spec,
        scratch_shapes=[
            pltpu.VMEM((2, ATTN_HEADS_PER_STEP, tk, tq), F32),
            pltpu.VMEM((ATTN_HEADS_PER_STEP, tk, tq), F32),
            pltpu.VMEM((ATTN_HEADS_PER_STEP, tk, tq), F32),
            pltpu.VMEM((nq, hg, tq), F32),
            pltpu.VMEM((ATTN_HEADS_PER_STEP, 1, tq), F32),
            pltpu.VMEM((ATTN_HEADS_PER_STEP, 1, tq), F32),
            pltpu.VMEM((2, tk, tq), F32),
        ],
    )
    return pl.pallas_call(
        functools.partial(_attn_kernel, dh=dh),
        out_shape=jax.ShapeDtypeStruct(qt.shape, BF16),
        grid_spec=grid_spec,
        compiler_params=_params(2),
        name="sb_attn",
    )(*_attn_tables(nq), qt, k, vt)


def _oproj_kernel(ot_ref, x_ref, mod_ref, w_ref, o_ref, *, sub):
    _, _, gate = _mod_rows(mod_ref, sub)
    tt = ot_ref.shape[-1]
    for j in range(ot_ref.shape[0]):
        y = lax.dot_general(ot_ref[j], w_ref[...], (((0,), (0,)), ((), ())),
                            preferred_element_type=F32)
        rows = slice(j * tt, (j + 1) * tt)
        o_ref[rows, :] = x_ref[rows, :] + (1.0 + gate) * y


def _oproj_call(ot, x, mod_l, w_o, sub):
    b, s, d = x.shape
    tt = ot.shape[-1]
    tm = 512
    assert s % tm == 0 and tm % tt == 0
    return pl.pallas_call(
        functools.partial(_oproj_kernel, sub=sub),
        out_shape=jax.ShapeDtypeStruct((b, s, d), F32),
        grid=(b, s // tm),
        in_specs=[
            pl.BlockSpec((None, tm // tt, d, tt), lambda bi, i: (bi, i, 0, 0)),
            pl.BlockSpec((None, tm, d), lambda bi, i: (bi, i, 0)),
            pl.BlockSpec((None, N_SUB * 3, d), lambda bi, i: (bi, 0, 0)),
            _resident((d, d), lambda bi, i: (0, 0)),
        ],
        out_specs=pl.BlockSpec((None, tm, d), lambda bi, i: (bi, i, 0)),
        compiler_params=_params(2),
        name="sb_oproj",
    )(ot, x, mod_l, w_o.astype(BF16))


def _lru_kernel(x_ref, mod_ref, g_ref, win_ref, cw_ref, cb_ref, wr_ref, br_ref, wi_ref, bi_ref,
                lam_ref, wout_ref, o_ref, xb_scr, hs_scr, ga_scr, gb_scr, hin_scr, h_scr, *, sub):
    ts = x_ref.shape[0]
    dr = cb_ref.shape[-1]
    n_taps = cw_ref.shape[0]
    grp = 8
    pad = grp
    n_slabs, n_grp = dr // LANES, ts // grp
    assert wr_ref.shape[-1] == LANES and n_taps - 1 <= pad

    @pl.when(pl.program_id(1) == 0)
    def _():
        xb_scr[:, 0:pad, :] = jnp.zeros((n_slabs, pad, LANES), F32)
        h_scr[...] = jnp.zeros_like(h_scr)

    x = x_ref[...]
    shift, scale, gate = _mod_rows(mod_ref, sub)
    h = _norm_mod(x, g_ref[sub:sub + 1, :], shift, scale).astype(BF16)
    gate_br = jnp.dot(h, win_ref[:, 0:dr], preferred_element_type=F32)
    xb = jnp.dot(h, win_ref[:, dr:2 * dr], preferred_element_type=F32)
    for n in range(n_slabs):
        xb_scr[n, pad:pad + ts, :] = xb[:, n * LANES:(n + 1) * LANES]

    nl = -lam_ref[...]
    softplus_nl = jnp.maximum(nl, 0.0) + jnp.log(1.0 + jnp.exp(-jnp.abs(nl)))

    cum = []
    for n in range(n_slabs):
        lanes = slice(n * LANES, (n + 1) * LANES)
        first = pad - (n_taps - 1)
        views = [xb_scr[n, pl.ds(o, n_grp, stride=grp), :] for o in range(first, pad + grp)]
        xc_steps = []
        for s in range(grp):
            acc = cb_ref[:, lanes]
            for k in range(n_taps):
                acc = acc + views[s + k] * cw_ref[k:k + 1, lanes]
            xc_steps.append(acc)
        xc = jnp.concatenate(xc_steps, axis=0)
        xcb = xc.astype(BF16)
        r = jax.nn.sigmoid(jnp.dot(xcb, wr_ref[n], preferred_element_type=F32) + br_ref[:, lanes])
        ig = jax.nn.sigmoid(jnp.dot(xcb, wi_ref[n], preferred_element_type=F32) + bi_ref[:, lanes])
        a = jnp.exp((-LRU_C) * r * softplus_nl[:, lanes])
        v = 1.0 - a * a
        b = (v * lax.rsqrt(jnp.maximum(v, TINY))) * (ig * xc)
        ca = cb = None
        per_step = []
        for s in range(grp):
            a_s, b_s = a[s * n_grp:(s + 1) * n_grp], b[s * n_grp:(s + 1) * n_grp]
            ca, cb = (a_s, b_s) if s == 0 else (a_s * ca, a_s * cb + b_s)
            per_step.append((ca, cb))
        ga_scr[n], gb_scr[n] = ca, cb
        cum.append(per_step)
    for n in range(n_slabs):
        xb_scr[n, 0:pad, :] = xb_scr[n, ts:ts + pad, :]

    def group_step(g, hs):
        out = []
        for n in range(n_slabs):
            hin_scr[n, pl.ds(g, 1), :] = hs[n]
            out.append(ga_scr[n, pl.ds(g, 1), :] * hs[n] + gb_scr[n, pl.ds(g, 1), :])
        return tuple(out)

    h0 = tuple(h_scr[:, n * LANES:(n + 1) * LANES] for n in range(n_slabs))
    h_end = lax.fori_loop(0, n_grp, group_step, h0)
    for n in range(n_slabs):
        h_scr[:, n * LANES:(n + 1) * LANES] = h_end[n]
        h_in = hin_scr[n]
        for s in range(grp):
            ca, cb = cum[n][s]
            hs_scr[n, pl.ds(s, n_grp, stride=grp), :] = ca * h_in + cb
    hs = jnp.concatenate([hs_scr[n] for n in range(n_slabs)], axis=1)

    y = (jax.nn.gelu(gate_br) * hs).astype(BF16)
    out = jnp.dot(y, wout_ref[...], preferred_element_type=F32)
    o_ref[...] = x + (1.0 + gate) * out


def _lru_call(x, mod_l, norm_g_l, w_in, conv_w, conv_b, w_r, b_r, w_i, b_i, lam, w_out, sub):
    b, s, d = x.shape
    dr = w_out.shape[0]
    nb, bw, _ = w_r.shape
    ts = 256
    assert s % ts == 0 and dr == nb * bw and bw % LANES == 0
    row = lambda v: v.reshape(1, dr)
    const2 = lambda bi, i: (0, 0)
    const3 = lambda bi, i: (0, 0, 0)
    return pl.pallas_call(
        functools.partial(_lru_kernel, sub=sub),
        out_shape=jax.ShapeDtypeStruct((b, s, d), F32),
        grid=(b, s // ts),
        in_specs=[
            pl.BlockSpec((None, ts, d), lambda bi, i: (bi, i, 0)),
            pl.BlockSpec((None, N_SUB * 3, d), lambda bi, i: (bi, 0, 0)),
            pl.BlockSpec((N_SUB, d), const2),
            _resident((d, 2 * dr), const2),
            pl.BlockSpec(conv_w.shape, const2),
            pl.BlockSpec((1, dr), const2),
            _resident((nb, bw, bw), const3),
            pl.BlockSpec((1, dr), const2),
            _resident((nb, bw, bw), const3),
            pl.BlockSpec((1, dr), const2),
            pl.BlockSpec((1, dr), const2),
            _resident((dr, d), const2),
        ],
        out_specs=pl.BlockSpec((None, ts, d), lambda bi, i: (bi, i, 0)),
        scratch_shapes=[
            pltpu.VMEM((dr // LANES, ts + 8, LANES), F32),
            pltpu.VMEM((dr // LANES, ts, LANES), F32),
            pltpu.VMEM((dr // LANES, ts // 8, LANES), F32),
            pltpu.VMEM((dr // LANES, ts // 8, LANES), F32),
            pltpu.VMEM((dr // LANES, ts // 8, LANES), F32),
            pltpu.VMEM((1, dr), F32),
        ],
        compiler_params=_params(2),
        name="rglru",
    )(x, mod_l, norm_g_l, w_in.astype(BF16), conv_w, row(conv_b), w_r.astype(BF16), row(b_r),
      w_i.astype(BF16), row(b_i), row(lam), w_out.astype(BF16))


def kernel(x, c, mod_w, mod_b, norm_g, ffn_w_gu, ffn_w_down, sb_w_qkv, sb_w_o, lru_w_in, lru_conv_w,
           lru_conv_b, lru_w_r, lru_b_r, lru_w_i, lru_b_i, lru_lambda, lru_w_out, final_norm_g):
    depth = mod_w.shape[0]
    b, s, d = x.shape
    mod = _mod_call(c, mod_w, mod_b).reshape(depth, b, N_SUB * 3, d)
    w_gu_all, w_down_all = ffn_w_gu.astype(BF16), ffn_w_down.astype(BF16)
    for layer in range(depth):
        mod_l, ng = mod[layer], norm_g[layer]
        x = _ffn_call(x, mod_l, ng, w_gu_all, w_down_all, layer, 0, sub=0)
        j = layer // 2
        if layer % 2 == 0:
            qt, k, vt = _qkv_call(x, mod_l, ng, sb_w_qkv[j], sub=1, tt=MXU_TILE)
            ot = _attn_call(qt, k, vt)
            x = _oproj_call(ot, x, mod_l, sb_w_o[j], sub=1)
        else:
            x = _lru_call(x, mod_l, ng, lru_w_in[j], lru_conv_w[j], lru_conv_b[j], lru_w_r[j],
                          lru_b_r[j], lru_w_i[j], lru_b_i[j], lru_lambda[j], lru_w_out[j], sub=1)
        last = layer == depth - 1
        x = _ffn_call(x, mod_l, ng, w_gu_all, w_down_all, layer, 1, sub=2,
                      final_g=final_norm_g if last else None)
    return x
```

```python
import functools

import jax
import jax.numpy as jnp
import numpy as np
from jax import lax
from jax.experimental import pallas as pl
from jax.experimental.pallas import tpu as pltpu

SB_HEADS = 16
LRU_C = 8.0
MACARON_W = 0.5
EPS = 1e-6
N_SUB = 3

LANES = 128
MXU_TILE = 256
VMEM_LIMIT_BYTES = 56 * 2 ** 20

ATTN_HEADS_PER_STEP = 8
NEG_BIG = -1e30
LOG2_E = 1.4426950408889634
TINY = 1e-30

F32 = jnp.float32
BF16 = jnp.bfloat16


def _params(n_grid_axes):
    return pltpu.CompilerParams(
        dimension_semantics=("arbitrary",) * n_grid_axes,
        vmem_limit_bytes=VMEM_LIMIT_BYTES,
    )


def _resident(block_shape, index_map):
    return pl.BlockSpec(block_shape, index_map, pipeline_mode=pl.Buffered(1))


def _norm_mod(x, g, shift, scale):
    inv = lax.rsqrt(jnp.mean(x * x, axis=-1, keepdims=True) + EPS)
    return (x * inv) * (g * (1.0 + scale)) + shift


def _mod_rows(mod_ref, sub):
    r = N_SUB * sub
    return mod_ref[r:r + 1, :], mod_ref[r + 1:r + 2, :], mod_ref[r + 2:r + 3, :]


def _mod_kernel(c_ref, w_ref, b_ref, o_ref):
    c = c_ref[...]
    ca = c * jax.nn.sigmoid(c)
    ca2 = jnp.concatenate([ca, ca], axis=0).astype(BF16)
    y = jnp.dot(ca2, w_ref[...].astype(BF16), preferred_element_type=F32)
    o_ref[...] = y[:c.shape[0]] + b_ref[...]


def _mod_call(c, mod_w, mod_b):
    depth, d, n = mod_w.shape
    b = c.shape[0]
    tn = 1024
    assert n % tn == 0
    return pl.pallas_call(
        _mod_kernel,
        out_shape=jax.ShapeDtypeStruct((depth, b, n), F32),
        grid=(depth, n // tn),
        in_specs=[
            pl.BlockSpec((b, d), lambda l, j: (0, 0)),
            pl.BlockSpec((None, d, tn), lambda l, j: (l, 0, j)),
            pl.BlockSpec((None, 1, tn), lambda l, j: (l, 0, j)),
        ],
        out_specs=pl.BlockSpec((None, b, tn), lambda l, j: (l, 0, j)),
        compiler_params=_params(2),
        name="mod",
    )(c, mod_w, mod_b.reshape(depth, 1, n))


def _ffn_kernel(x_ref, mod_ref, g_ref, wgu_ref, wd_ref, *rest, sub, final):
    if final:
        fg_ref, o_ref = rest
    else:
        (o_ref,) = rest
    x = x_ref[...]
    d_ff = wd_ref.shape[0]
    tf = MXU_TILE
    shift, scale, gate = _mod_rows(mod_ref, sub)
    h = _norm_mod(x, g_ref[sub:sub + 1, :], shift, scale).astype(BF16)
    acts = []
    for c in range(d_ff // tf):
        g = jnp.dot(h, wgu_ref[:, c * tf:(c + 1) * tf], preferred_element_type=F32)
        u = jnp.dot(h, wgu_ref[:, d_ff + c * tf:d_ff + (c + 1) * tf], preferred_element_type=F32)
        acts.append((g * jax.nn.sigmoid(g) * u).astype(BF16))
    y = jnp.dot(jnp.concatenate(acts, axis=1), wd_ref[...], preferred_element_type=F32)
    out = x + (MACARON_W * (1.0 + gate)) * y
    if final:
        inv = lax.rsqrt(jnp.mean(out * out, axis=-1, keepdims=True) + EPS)
        out = (out * inv) * fg_ref[...]
    o_ref[...] = out


def _ffn_call(x, mod_l, norm_g_l, w_gu_all, w_down_all, layer, half, sub, final_g=None):
    b, s, d = x.shape
    d_ff = w_down_all.shape[-2]
    tm = 512
    assert s % tm == 0 and d_ff % MXU_TILE == 0
    final = final_g is not None
    in_specs = [
        pl.BlockSpec((None, tm, d), lambda bi, i: (bi, i, 0)),
        pl.BlockSpec((None, N_SUB * 3, d), lambda bi, i: (bi, 0, 0)),
        pl.BlockSpec((N_SUB, d), lambda bi, i: (0, 0)),
        _resident((None, None, d, 2 * d_ff), lambda bi, i: (layer, half, 0, 0)),
        _resident((None, None, d_ff, d), lambda bi, i: (layer, half, 0, 0)),
    ]
    args = [x, mod_l, norm_g_l, w_gu_all, w_down_all]
    if final:
        in_specs.append(pl.BlockSpec((1, d), lambda bi, i: (0, 0)))
        args.append(final_g.reshape(1, d))
    return pl.pallas_call(
        functools.partial(_ffn_kernel, sub=sub, final=final),
        out_shape=jax.ShapeDtypeStruct((b, s, d), F32),
        grid=(b, s // tm),
        in_specs=in_specs,
        out_specs=pl.BlockSpec((None, tm, d), lambda bi, i: (bi, i, 0)),
        compiler_params=_params(2),
        name="ffn_final" if final else "ffn",
    )(*args)


def _qkv_kernel(x_ref, mod_ref, g_ref, w_ref, qt_ref, k_ref, vt_ref, *, sub, scale_q):
    d = x_ref.shape[-1]
    shift, scale, _ = _mod_rows(mod_ref, sub)
    h = _norm_mod(x_ref[...], g_ref[sub:sub + 1, :], shift, scale).astype(BF16)
    q = jnp.dot(h, w_ref[:, 0:d], preferred_element_type=F32) * scale_q
    k_ref[...] = jnp.dot(h, w_ref[:, d:2 * d], preferred_element_type=F32).astype(BF16)
    v = jnp.dot(h, w_ref[:, 2 * d:3 * d], preferred_element_type=F32)
    tt = vt_ref.shape[-1]
    for j in range(vt_ref.shape[0]):
        qt_ref[j] = q[j * tt:(j + 1) * tt, :].T.astype(BF16)
        vt_ref[j] = v[j * tt:(j + 1) * tt, :].T.astype(BF16)


def _qkv_call(x, mod_l, norm_g_l, w_qkv, sub, tt):
    b, s, d = x.shape
    tm = 512
    assert s % tm == 0 and tm % tt == 0
    dh = d // SB_HEADS
    tiled_t = jax.ShapeDtypeStruct((b, s // tt, d, tt), BF16)
    tiled_spec = pl.BlockSpec((None, tm // tt, d, tt), lambda bi, i: (bi, i, 0, 0))
    return pl.pallas_call(
        functools.partial(_qkv_kernel, sub=sub, scale_q=dh ** -0.5 * LOG2_E),
        out_shape=(tiled_t,
                   jax.ShapeDtypeStruct((b, s, d), BF16),
                   tiled_t),
        grid=(b, s // tm),
        in_specs=[
            pl.BlockSpec((None, tm, d), lambda bi, i: (bi, i, 0)),
            pl.BlockSpec((None, N_SUB * 3, d), lambda bi, i: (bi, 0, 0)),
            pl.BlockSpec((N_SUB, d), lambda bi, i: (0, 0)),
            _resident((d, 3 * d), lambda bi, i: (0, 0)),
        ],
        out_specs=(tiled_spec, pl.BlockSpec((None, tm, d), lambda bi, i: (bi, i, 0)), tiled_spec),
        compiler_params=_params(2),
        name="qkv",
    )(x, mod_l, norm_g_l, w_qkv.astype(BF16))


def _attn_tables(n_qblocks):
    qb = [0, 0] + list(range(n_qblocks))
    kb = [0, 0] + list(range(n_qblocks))
    dg = [1, 1] + [1] * n_qblocks
    for i in range(n_qblocks):
        for t in range(1, i + 1):
            qb.append(i)
            kb.append(i - t)
            dg.append(0)
    n_back = 2 + len(qb) % 2
    qb, kb, dg = qb + [0] * n_back, kb + [0] * n_back, dg + [1] * n_back
    return tuple(jnp.asarray(np.asarray(v, np.int32)) for v in (qb, kb, dg))


def _attn_kernel(qb_tab, kb_tab, dg_tab, qt_ref, k_ref, vt_ref, ot_ref,
                 z_scr, lb_scr, later_scr, acc_scr, tail_scr, tail_at_scr, cap_scr, *, dh):
    tq = qt_ref.shape[-1]
    tk = vt_ref.shape[-1]
    hp = 2 * dh
    n_heads = qt_ref.shape[1] // dh
    n_iter = qb_tab.shape[0] - 2
    n_diag = qt_ref.shape[0]
    assert n_diag % 2 == 0 and n_iter % 2 == 0

    kr = lax.broadcasted_iota(jnp.int32, (tk, tq), 0)
    qc = lax.broadcasted_iota(jnp.int32, (tk, tq), 1)
    cap_scr[...] = jnp.where(kr < qc, jnp.inf, NEG_BIG).astype(F32)
    z_scr[1] = jnp.full(z_scr.shape[1:], NEG_BIG, F32)
    lb_scr[...] = jnp.full(lb_scr.shape, NEG_BIG, F32)
    later_scr[...] = jnp.zeros(later_scr.shape, F32)
    acc_scr[...] = jnp.zeros(acc_scr.shape, F32)
    tail_scr[...] = jnp.zeros(tail_scr.shape, F32)
    tail_at_scr[...] = jnp.zeros(tail_at_scr.shape, F32)

    jr = lax.broadcasted_iota(jnp.int32, (tk, tk), 0)
    jc = lax.broadcasted_iota(jnp.int32, (tk, tk), 1)
    later_mat = (jc > jr).astype(BF16)
    frow = lax.broadcasted_iota(jnp.int32, (hp, tq), 0)

    def stages(m, slot, diagonal):
        qb_s, kb_s = qb_tab[m + 2], kb_tab[m + 2]
        qb_m, dg_m = qb_tab[m + 1], dg_tab[m + 1]
        qb_o, kb_o = qb_tab[m], kb_tab[m]
        keep_tail = jnp.where(dg_m == 1, 0.0, 1.0)
        k_all = k_ref[pl.ds(pl.multiple_of(kb_s * tk, tk), tk), :]
        qt_all = qt_ref[qb_s]
        vt_all = vt_ref[kb_o]
        for hd in range(n_heads):
            p, sub = divmod(hd, 2)
            pair = slice(p * hp, (p + 1) * hp)
            w = jnp.exp2(lb_scr[hd] + later_scr[hd]).astype(BF16)
            pv = jnp.dot(vt_all[pair, :], w, preferred_element_type=F32)
            rows = slice(hd * dh, (hd + 1) * dh)
            acc_scr[qb_o, rows, :] += pv[sub * dh:(sub + 1) * dh, :] * jnp.exp2(tail_at_scr[hd])
            zz = z_scr[1 - slot, hd]
            log_beta = jnp.minimum(zz, 0.0) - jnp.log(1.0 + jnp.exp2(-jnp.abs(zz))) * LOG2_E
            log_keep = log_beta - zz
            lb_scr[hd] = log_beta
            later = jnp.dot(later_mat, log_keep.astype(BF16), preferred_element_type=F32)
            later_scr[hd] = later
            tail_at = tail_scr[qb_m, hd] * keep_tail
            tail_at_scr[hd] = tail_at
            tail_scr[qb_m, hd] = tail_at + later[0:1, :] + log_keep[0:1, :]
            qt = qt_all[pair, :]
            qt = jnp.where((frow >= sub * dh) & (frow < (sub + 1) * dh), qt, jnp.zeros_like(qt))
            z = jnp.dot(k_all[:, pair], qt, preferred_element_type=F32)
            z_scr[slot, hd] = jnp.minimum(z, cap_scr[...]) if diagonal else z

    def body(i, carry, diagonal):
        stages(2 * i, 0, diagonal)
        stages(2 * i + 1, 1, diagonal)
        return carry

    lax.fori_loop(0, n_diag // 2, functools.partial(body, diagonal=True), 0)
    lax.fori_loop(n_diag // 2, n_iter // 2, functools.partial(body, diagonal=False), 0)
    ot_ref[...] = acc_scr[...].astype(BF16)


def _attn_call(qt, k, vt):
    b, nq, d, tq = qt.shape
    tk = vt.shape[-1]
    s = k.shape[1]
    dh = d // SB_HEADS
    assert 2 * dh == LANES and tq == tk and nq * tq == s
    hg = ATTN_HEADS_PER_STEP * dh
    assert ATTN_HEADS_PER_STEP % 2 == 0 and d % hg == 0
    tiled_spec = pl.BlockSpec((None, nq, hg, tq), lambda bi, g, *_: (bi, 0, g, 0))
    grid_spec = pltpu.PrefetchScalarGridSpec(
        num_scalar_prefetch=3,
        grid=(b, d // hg),
        in_specs=[tiled_spec, pl.BlockSpec((None, s, hg), lambda bi, g, *_: (bi, 0, g)), tiled_spec],
        out_specs=tiled_spec,
        scratch_shapes=[
            pltpu.VMEM((2, ATTN_HEADS_PER_STEP, tk, tq), F32),
            pltpu.VMEM((ATTN_HEADS_PER_STEP, tk, tq), F32),
            pltpu.VMEM((ATTN_HEADS_PER_STEP, tk, tq), F32),
            pltpu.VMEM((nq, hg, tq), F32),
            pltpu.VMEM((nq, ATTN_HEADS_PER_STEP, 1, tq), F32),
            pltpu.VMEM((ATTN_HEADS_PER_STEP, 1, tq), F32),
            pltpu.VMEM((tk, tq), F32),
        ],
    )
    return pl.pallas_call(
        functools.partial(_attn_kernel, dh=dh),
        out_shape=jax.ShapeDtypeStruct(qt.shape, BF16),
        grid_spec=grid_spec,
        compiler_params=_params(2),
        name="sb_attn",
    )(*_attn_tables(nq), qt, k, vt)


def _oproj_kernel(ot_ref, x_ref, mod_ref, w_ref, o_ref, *, sub):
    _, _, gate = _mod_rows(mod_ref, sub)
    tt = ot_ref.shape[-1]
    for j in range(ot_ref.shape[0]):
        y = lax.dot_general(ot_ref[j], w_ref[...], (((0,), (0,)), ((), ())),
                            preferred_element_type=F32)
        rows = slice(j * tt, (j + 1) * tt)
        o_ref[rows, :] = x_ref[rows, :] + (1.0 + gate) * y


def _oproj_call(ot, x, mod_l, w_o, sub):
    b, s, d = x.shape
    tt = ot.shape[-1]
    tm = 512
    assert s % tm == 0 and tm % tt == 0
    return pl.pallas_call(
        functools.partial(_oproj_kernel, sub=sub),
        out_shape=jax.ShapeDtypeStruct((b, s, d), F32),
        grid=(b, s // tm),
        in_specs=[
            pl.BlockSpec((None, tm // tt, d, tt), lambda bi, i: (bi, i, 0, 0)),
            pl.BlockSpec((None, tm, d), lambda bi, i: (bi, i, 0)),
            pl.BlockSpec((None, N_SUB * 3, d), lambda bi, i: (bi, 0, 0)),
            _resident((d, d), lambda bi, i: (0, 0)),
        ],
        out_specs=pl.BlockSpec((None, tm, d), lambda bi, i: (bi, i, 0)),
        compiler_params=_params(2),
        name="sb_oproj",
    )(ot, x, mod_l, w_o.astype(BF16))


def _lru_kernel(x_ref, mod_ref, g_ref, win_ref, cw_ref, cb_ref, wr_ref, br_ref, wi_ref, bi_ref,
                lam_ref, wout_ref, o_ref, xb_scr, hs_scr, ga_scr, gb_scr, hin_scr, h_scr, *, sub, n_sub):
    ts = x_ref.shape[0]
    dr = cb_ref.shape[-1]
    n_taps = cw_ref.shape[0]
    grp = 8
    pad = grp
    rows_sub = ts // n_sub
    n_slabs, n_grp = dr // LANES, rows_sub // grp
    first = pad - (n_taps - 1)
    assert wr_ref.shape[-1] == LANES and first >= 0

    @pl.when(pl.program_id(1) == 0)
    def _():
        xb_scr[:, 0:pad, :] = jnp.zeros((n_slabs, pad, LANES), F32)
        h_scr[...] = jnp.zeros_like(h_scr)

    x = x_ref[...]
    shift, scale, gate = _mod_rows(mod_ref, sub)
    h = _norm_mod(x, g_ref[sub:sub + 1, :], shift, scale).astype(BF16)
    nl = -lam_ref[...]
    softplus_nl = jnp.maximum(nl, 0.0) + jnp.log(1.0 + jnp.exp(-jnp.abs(nl)))
    gate_br = [None] * n_sub
    hs_sub = [None] * n_sub
    state = {"h": tuple(h_scr[:, n * LANES:(n + 1) * LANES] for n in range(n_slabs))}

    def x_proj(j):
        xb = jnp.dot(h[j * rows_sub:(j + 1) * rows_sub], win_ref[:, dr:2 * dr], preferred_element_type=F32)
        for n in range(n_slabs):
            xb_scr[n, pad + j * rows_sub:pad + (j + 1) * rows_sub, :] = xb[:, n * LANES:(n + 1) * LANES]

    def gate_proj(j):
        gate_br[j] = jnp.dot(h[j * rows_sub:(j + 1) * rows_sub], win_ref[:, 0:dr],
                             preferred_element_type=F32)

    def middle(j):
        base = j * rows_sub
        cum = []
        for n in range(n_slabs):
            lanes = slice(n * LANES, (n + 1) * LANES)
            views = [xb_scr[n, pl.ds(base + o, n_grp, stride=grp), :] for o in range(first, pad + grp)]
            xc_steps = []
            for s in range(grp):
                acc = cb_ref[:, lanes]
                for k in range(n_taps):
                    acc = acc + views[s + k] * cw_ref[k:k + 1, lanes]
                xc_steps.append(acc)
            xc = jnp.concatenate(xc_steps, axis=0)
            xcb = xc.astype(BF16)
            r = jax.nn.sigmoid(jnp.dot(xcb, wr_ref[n], preferred_element_type=F32) + br_ref[:, lanes])
            ig = jax.nn.sigmoid(jnp.dot(xcb, wi_ref[n], preferred_element_type=F32) + bi_ref[:, lanes])
            a = jnp.exp((-LRU_C) * r * softplus_nl[:, lanes])
            v = 1.0 - a * a
            b = (v * lax.rsqrt(jnp.maximum(v, TINY))) * (ig * xc)
            ca = cb = None
            per_step = []
            for s in range(grp):
                a_s, b_s = a[s * n_grp:(s + 1) * n_grp], b[s * n_grp:(s + 1) * n_grp]
                ca, cb = (a_s, b_s) if s == 0 else (a_s * ca, a_s * cb + b_s)
                per_step.append((ca, cb))
            ga_scr[j, n], gb_scr[j, n] = ca, cb
            cum.append(per_step)
        hcur = list(state["h"])
        for g in range(n_grp):
            for n in range(n_slabs):
                hin_scr[j, n, g:g + 1, :] = hcur[n]
                hcur[n] = ga_scr[j, n, g:g + 1, :] * hcur[n] + gb_scr[j, n, g:g + 1, :]
        state["h"] = tuple(hcur)
        for n in range(n_slabs):
            h_in = hin_scr[j, n]
            for s in range(grp):
                ca, cb = cum[n][s]
                hs_scr[n, pl.ds(base + s, n_grp, stride=grp), :] = ca * h_in + cb
        hs_sub[j] = jnp.concatenate([hs_scr[n, base:base + rows_sub, :] for n in range(n_slabs)], axis=1)

    def out_proj(j):
        rows = slice(j * rows_sub, (j + 1) * rows_sub)
        y = (jax.nn.gelu(gate_br[j]) * hs_sub[j]).astype(BF16)
        out = jnp.dot(y, wout_ref[...], preferred_element_type=F32)
        o_ref[rows, :] = x[rows] + (1.0 + gate) * out

    x_proj(0)
    for j in range(n_sub):
        if j + 1 < n_sub:
            x_proj(j + 1)
        gate_proj(j)
        if j >= 1:
            out_proj(j - 1)
        middle(j)
    out_proj(n_sub - 1)
    for n in range(n_slabs):
        h_scr[:, n * LANES:(n + 1) * LANES] = state["h"][n]
        xb_scr[n, 0:pad, :] = xb_scr[n, ts:ts + pad, :]


def _lru_call(x, mod_l, norm_g_l, w_in, conv_w, conv_b, w_r, b_r, w_i, b_i, lam, w_out, sub):
    b, s, d = x.shape
    dr = w_out.shape[0]
    nb, bw, _ = w_r.shape
    ts, n_sub = 512, 4
    n_grp = ts // n_sub // 8
    assert s % ts == 0 and dr == nb * bw and bw == LANES
    row = lambda v: v.reshape(1, dr)
    const2 = lambda bi, i: (0, 0)
    const3 = lambda bi, i: (0, 0, 0)
    return pl.pallas_call(
        functools.partial(_lru_kernel, sub=sub, n_sub=n_sub),
        out_shape=jax.ShapeDtypeStruct((b, s, d), F32),
        grid=(b, s // ts),
        in_specs=[
            pl.BlockSpec((None, ts, d), lambda bi, i: (bi, i, 0)),
            pl.BlockSpec((None, N_SUB * 3, d), lambda bi, i: (bi, 0, 0)),
            pl.BlockSpec((N_SUB, d), const2),
            _resident((d, 2 * dr), const2),
            pl.BlockSpec(conv_w.shape, const2),
            pl.BlockSpec((1, dr), const2),
            _resident((nb, bw, bw), const3),
            pl.BlockSpec((1, dr), const2),
            _resident((nb, bw, bw), const3),
            pl.BlockSpec((1, dr), const2),
            pl.BlockSpec((1, dr), const2),
            _resident((dr, d), const2),
        ],
        out_specs=pl.BlockSpec((None, ts, d), lambda bi, i: (bi, i, 0)),
        scratch_shapes=[
            pltpu.VMEM((dr // LANES, ts + 8, LANES), F32),
            pltpu.VMEM((dr // LANES, ts, LANES), F32),
            pltpu.VMEM((n_sub, dr // LANES, n_grp, LANES), F32),
            pltpu.VMEM((n_sub, dr // LANES, n_grp, LANES), F32),
            pltpu.VMEM((n_sub, dr // LANES, n_grp, LANES), F32),
            pltpu.VMEM((1, dr), F32),
        ],
        compiler_params=_params(2),
        name="rglru",
    )(x, mod_l, norm_g_l, w_in.astype(BF16), conv_w, row(conv_b), w_r.astype(BF16), row(b_r),
      w_i.astype(BF16), row(b_i), row(lam), w_out.astype(BF16))


def kernel(x, c, mod_w, mod_b, norm_g, ffn_w_gu, ffn_w_down, sb_w_qkv, sb_w_o, lru_w_in, lru_conv_w,
           lru_conv_b, lru_w_r, lru_b_r, lru_w_i, lru_b_i, lru_lambda, lru_w_out, final_norm_g):
    depth = mod_w.shape[0]
    b, s, d = x.shape
    mod = _mod_call(c, mod_w, mod_b).reshape(depth, b, N_SUB * 3, d)
    w_gu_all, w_down_all = ffn_w_gu.astype(BF16), ffn_w_down.astype(BF16)
    for layer in range(depth):
        mod_l, ng = mod[layer], norm_g[layer]
        x = _ffn_call(x, mod_l, ng, w_gu_all, w_down_all, layer, 0, sub=0)
        j = layer // 2
        if layer % 2 == 0:
            qt, k, vt = _qkv_call(x, mod_l, ng, sb_w_qkv[j], sub=1, tt=MXU_TILE)
            ot = _attn_call(qt, k, vt)
            x = _oproj_call(ot, x, mod_l, sb_w_o[j], sub=1)
        else:
            x = _lru_call(x, mod_l, ng, lru_w_in[j], lru_conv_w[j], lru_conv_b[j], lru_w_r[j],
                          lru_b_r[j], lru_w_i[j], lru_b_i[j], lru_lambda[j], lru_w_out[j], sub=1)
        last = layer == depth - 1
        x = _ffn_call(x, mod_l, ng, w_gu_all, w_down_all, layer, 1, sub=2,
                      final_g=final_norm_g if last else None)
    return x
```

```python
import functools

import jax
import jax.numpy as jnp
import numpy as np
from jax import lax
from jax.experimental import pallas as pl
from jax.experimental.pallas import tpu as pltpu

SB_HEADS = 16
LRU_C = 8.0
MACARON_W = 0.5
EPS = 1e-6
N_SUB = 3

LANES = 128
MXU_TILE = 256
VMEM_LIMIT_BYTES = 56 * 2 ** 20

ATTN_HEADS_PER_STEP = 8
NEG_BIG = -1e30
LOG2_E = 1.4426950408889634
TINY = 1e-30

F32 = jnp.float32
BF16 = jnp.bfloat16


def _params(n_grid_axes):
    return pltpu.CompilerParams(
        dimension_semantics=("arbitrary",) * n_grid_axes,
        vmem_limit_bytes=VMEM_LIMIT_BYTES,
    )


def _resident(block_shape, index_map):
    return pl.BlockSpec(block_shape, index_map, pipeline_mode=pl.Buffered(1))


def _norm_mod(x, g, shift, scale):
    inv = lax.rsqrt(jnp.mean(x * x, axis=-1, keepdims=True) + EPS)
    return (x * inv) * (g * (1.0 + scale)) + shift


def _mod_rows(mod_ref, sub):
    r = N_SUB * sub
    return mod_ref[r:r + 1, :], mod_ref[r + 1:r + 2, :], mod_ref[r + 2:r + 3, :]


def _mod_kernel(c_ref, w_ref, b_ref, o_ref):
    c = c_ref[...]
    ca = c * jax.nn.sigmoid(c)
    ca2 = jnp.concatenate([ca, ca], axis=0).astype(BF16)
    y = jnp.dot(ca2, w_ref[...].astype(BF16), preferred_element_type=F32)
    o_ref[...] = y[:c.shape[0]] + b_ref[...]


def _mod_call(c, mod_w, mod_b):
    depth, d, n = mod_w.shape
    b = c.shape[0]
    tn = 1024
    assert n % tn == 0
    return pl.pallas_call(
        _mod_kernel,
        out_shape=jax.ShapeDtypeStruct((depth, b, n), F32),
        grid=(depth, n // tn),
        in_specs=[
            pl.BlockSpec((b, d), lambda l, j: (0, 0)),
            pl.BlockSpec((None, d, tn), lambda l, j: (l, 0, j)),
            pl.BlockSpec((None, 1, tn), lambda l, j: (l, 0, j)),
        ],
        out_specs=pl.BlockSpec((None, b, tn), lambda l, j: (l, 0, j)),
        compiler_params=_params(2),
        name="mod",
    )(c, mod_w, mod_b.reshape(depth, 1, n))


def _ffn_kernel(x_ref, mod_ref, g_ref, wgu_ref, wd_ref, *rest, sub, final, mix_sub, pre_oproj, post_qkv,
                scale_q):
    rest = list(rest)
    if pre_oproj:
        ot_ref, wo_ref = rest[:2]
        rest = rest[2:]
    if post_qkv:
        wqkv_ref = rest.pop(0)
    if final:
        fg_ref = rest.pop(0)
    o_ref = rest.pop(0)
    if pre_oproj:
        x = _oproj_residual(ot_ref, x_ref, mod_ref, wo_ref, mix_sub)
    else:
        x = x_ref[...]
    d_ff = wd_ref.shape[0]
    tf = MXU_TILE
    shift, scale, gate = _mod_rows(mod_ref, sub)
    h = _norm_mod(x, g_ref[sub:sub + 1, :], shift, scale).astype(BF16)
    acts = []
    for c in range(d_ff // tf):
        g = jnp.dot(h, wgu_ref[:, c * tf:(c + 1) * tf], preferred_element_type=F32)
        u = jnp.dot(h, wgu_ref[:, d_ff + c * tf:d_ff + (c + 1) * tf], preferred_element_type=F32)
        acts.append((g * jax.nn.sigmoid(g) * u).astype(BF16))
    y = jnp.dot(jnp.concatenate(acts, axis=1), wd_ref[...], preferred_element_type=F32)
    out = x + (MACARON_W * (1.0 + gate)) * y
    if final:
        inv = lax.rsqrt(jnp.mean(out * out, axis=-1, keepdims=True) + EPS)
        out = (out * inv) * fg_ref[...]
    o_ref[...] = out
    if post_qkv:
        _qkv_emit(out, mod_ref, g_ref, wqkv_ref, *rest, sub=mix_sub, scale_q=scale_q)


def _oproj_residual(ot_ref, x_ref, mod_ref, w_ref, sub):
    _, _, gate = _mod_rows(mod_ref, sub)
    tt = ot_ref.shape[-1]
    parts = []
    for j in range(ot_ref.shape[0]):
        y = lax.dot_general(ot_ref[j], w_ref[...], (((0,), (0,)), ((), ())),
                            preferred_element_type=F32)
        parts.append(x_ref[j * tt:(j + 1) * tt, :] + (1.0 + gate) * y)
    return jnp.concatenate(parts, axis=0)


def _qkv_emit(x, mod_ref, g_ref, w_ref, qt_ref, k_ref, vt_ref, *, sub, scale_q):
    d = x.shape[-1]
    shift, scale, _ = _mod_rows(mod_ref, sub)
    h = _norm_mod(x, g_ref[sub:sub + 1, :], shift, scale).astype(BF16)
    q = jnp.dot(h, w_ref[:, 0:d], preferred_element_type=F32) * scale_q
    k_ref[...] = jnp.dot(h, w_ref[:, d:2 * d], preferred_element_type=F32).astype(BF16)
    v = jnp.dot(h, w_ref[:, 2 * d:3 * d], preferred_element_type=F32)
    tt = vt_ref.shape[-1]
    for j in range(vt_ref.shape[0]):
        qt_ref[j] = q[j * tt:(j + 1) * tt, :].T.astype(BF16)
        vt_ref[j] = v[j * tt:(j + 1) * tt, :].T.astype(BF16)


def _ffn_call(x, mod_l, norm_g_l, w_gu_all, w_down_all, layer, half, sub, final_g=None,
              attn_out=None, w_o=None, w_qkv=None, mix_sub=1, tt=MXU_TILE):
    b, s, d = x.shape
    d_ff = w_down_all.shape[-2]
    tm = 512
    assert s % tm == 0 and d_ff % MXU_TILE == 0 and tm % tt == 0
    final, pre_oproj, post_qkv = final_g is not None, attn_out is not None, w_qkv is not None
    tile_spec = pl.BlockSpec((None, tm, d), lambda bi, i: (bi, i, 0))
    tiled_t_spec = pl.BlockSpec((None, tm // tt, d, tt), lambda bi, i: (bi, i, 0, 0))
    in_specs = [
        tile_spec,
        pl.BlockSpec((None, N_SUB * 3, d), lambda bi, i: (bi, 0, 0)),
        pl.BlockSpec((N_SUB, d), lambda bi, i: (0, 0)),
        _resident((None, None, d, 2 * d_ff), lambda bi, i: (layer, half, 0, 0)),
        _resident((None, None, d_ff, d), lambda bi, i: (layer, half, 0, 0)),
    ]
    args = [x, mod_l, norm_g_l, w_gu_all, w_down_all]
    if pre_oproj:
        in_specs += [tiled_t_spec, _resident((d, d), lambda bi, i: (0, 0))]
        args += [attn_out, w_o.astype(BF16)]
    if post_qkv:
        in_specs.append(_resident((d, 3 * d), lambda bi, i: (0, 0)))
        args.append(w_qkv.astype(BF16))
    if final:
        in_specs.append(pl.BlockSpec((1, d), lambda bi, i: (0, 0)))
        args.append(final_g.reshape(1, d))
    out_shape = jax.ShapeDtypeStruct((b, s, d), F32)
    out_specs = tile_spec
    if post_qkv:
        tiled_t = jax.ShapeDtypeStruct((b, s // tt, d, tt), BF16)
        out_shape = (out_shape, tiled_t, jax.ShapeDtypeStruct((b, s, d), BF16), tiled_t)
        out_specs = (tile_spec, tiled_t_spec, tile_spec, tiled_t_spec)
    dh = d // SB_HEADS
    return pl.pallas_call(
        functools.partial(_ffn_kernel, sub=sub, final=final, mix_sub=mix_sub, pre_oproj=pre_oproj,
                          post_qkv=post_qkv, scale_q=dh ** -0.5 * LOG2_E),
        out_shape=out_shape,
        grid=(b, s // tm),
        in_specs=in_specs,
        out_specs=out_specs,
        compiler_params=_params(2),
        name="ffn" + ("_oproj" if pre_oproj else "") + ("_qkv" if post_qkv else "") + ("_final" if final else ""),
    )(*args)


def _attn_tables(n_qblocks):
    qb = [0, 0] + list(range(n_qblocks))
    kb = [0, 0] + list(range(n_qblocks))
    dg = [1, 1] + [1] * n_qblocks
    for i in range(n_qblocks):
        for t in range(1, i + 1):
            qb.append(i)
            kb.append(i - t)
            dg.append(0)
    n_back = 2 + len(qb) % 2
    qb, kb, dg = qb + [0] * n_back, kb + [0] * n_back, dg + [1] * n_back
    return tuple(jnp.asarray(np.asarray(v, np.int32)) for v in (qb, kb, dg))


def _attn_kernel(qb_tab, kb_tab, dg_tab, qt_ref, k_ref, vt_ref, ot_ref,
                 z_scr, lb_scr, later_scr, acc_scr, tail_scr, tail_at_scr, cap_scr, *, dh):
    tq = qt_ref.shape[-1]
    tk = vt_ref.shape[-1]
    hp = 2 * dh
    n_heads = qt_ref.shape[1] // dh
    n_iter = qb_tab.shape[0] - 2
    n_diag = qt_ref.shape[0]
    assert n_diag % 2 == 0 and n_iter % 2 == 0

    kr = lax.broadcasted_iota(jnp.int32, (tk, tq), 0)
    qc = lax.broadcasted_iota(jnp.int32, (tk, tq), 1)
    cap_scr[...] = jnp.where(kr < qc, jnp.inf, NEG_BIG).astype(F32)
    z_scr[1] = jnp.full(z_scr.shape[1:], NEG_BIG, F32)
    lb_scr[...] = jnp.full(lb_scr.shape, NEG_BIG, F32)
    later_scr[...] = jnp.zeros(later_scr.shape, F32)
    acc_scr[...] = jnp.zeros(acc_scr.shape, F32)
    tail_scr[...] = jnp.zeros(tail_scr.shape, F32)
    tail_at_scr[...] = jnp.zeros(tail_at_scr.shape, F32)

    jr = lax.broadcasted_iota(jnp.int32, (tk, tk), 0)
    jc = lax.broadcasted_iota(jnp.int32, (tk, tk), 1)
    later_mat = (jc > jr).astype(BF16)
    frow = lax.broadcasted_iota(jnp.int32, (hp, tq), 0)

    def stages(m, slot, diagonal):
        qb_s, kb_s = qb_tab[m + 2], kb_tab[m + 2]
        qb_m, dg_m = qb_tab[m + 1], dg_tab[m + 1]
        qb_o, kb_o = qb_tab[m], kb_tab[m]
        keep_tail = jnp.where(dg_m == 1, 0.0, 1.0)
        k_all = k_ref[pl.ds(pl.multiple_of(kb_s * tk, tk), tk), :]
        qt_all = qt_ref[qb_s]
        vt_all = vt_ref[kb_o]
        for hd in range(n_heads):
            p, sub = divmod(hd, 2)
            pair = slice(p * hp, (p + 1) * hp)
            w = jnp.exp2(lb_scr[hd] + later_scr[hd]).astype(BF16)
            pv = jnp.dot(vt_all[pair, :], w, preferred_element_type=F32)
            rows = slice(hd * dh, (hd + 1) * dh)
            acc_scr[qb_o, rows, :] += pv[sub * dh:(sub + 1) * dh, :] * jnp.exp2(tail_at_scr[hd])
            zz = z_scr[1 - slot, hd]
            log_beta = jnp.minimum(zz, 0.0) - jnp.log(1.0 + jnp.exp2(-jnp.abs(zz))) * LOG2_E
            log_keep = log_beta - zz
            lb_scr[hd] = log_beta
            later = jnp.dot(later_mat, log_keep.astype(BF16), preferred_element_type=F32)
            later_scr[hd] = later
            tail_at = tail_scr[qb_m, hd] * keep_tail
            tail_at_scr[hd] = tail_at
            tail_scr[qb_m, hd] = tail_at + later[0:1, :] + log_keep[0:1, :]
            qt = qt_all[pair, :]
            qt = jnp.where((frow >= sub * dh) & (frow < (sub + 1) * dh), qt, jnp.zeros_like(qt))
            z = jnp.dot(k_all[:, pair], qt, preferred_element_type=F32)
            z_scr[slot, hd] = jnp.minimum(z, cap_scr[...]) if diagonal else z

    def body(i, carry, diagonal):
        stages(2 * i, 0, diagonal)
        stages(2 * i + 1, 1, diagonal)
        return carry

    lax.fori_loop(0, n_diag // 2, functools.partial(body, diagonal=True), 0)
    lax.fori_loop(n_diag // 2, n_iter // 2, functools.partial(body, diagonal=False), 0)
    ot_ref[...] = acc_scr[...].astype(BF16)


def _attn_call(qt, k, vt):
    b, nq, d, tq = qt.shape
    tk = vt.shape[-1]
    s = k.shape[1]
    dh = d // SB_HEADS
    assert 2 * dh == LANES and tq == tk and nq * tq == s
    hg = ATTN_HEADS_PER_STEP * dh
    assert ATTN_HEADS_PER_STEP % 2 == 0 and d % hg == 0
    tiled_spec = pl.BlockSpec((None, nq, hg, tq), lambda bi, g, *_: (bi, 0, g, 0))
    grid_spec = pltpu.PrefetchScalarGridSpec(
        num_scalar_prefetch=3,
        grid=(b, d // hg),
        in_specs=[tiled_spec, pl.BlockSpec((None, s, hg), lambda bi, g, *_: (bi, 0, g)), tiled_spec],
        out_specs=tiled_spec,
        scratch_shapes=[
            pltpu.VMEM((2, ATTN_HEADS_PER_STEP, tk, tq), F32),
            pltpu.VMEM((ATTN_HEADS_PER_STEP, tk, tq), F32),
            pltpu.VMEM((ATTN_HEADS_PER_STEP, tk, tq), F32),
            pltpu.VMEM((nq, hg, tq), F32),
            pltpu.VMEM((nq, ATTN_HEADS_PER_STEP, 1, tq), F32),
            pltpu.VMEM((ATTN_HEADS_PER_STEP, 1, tq), F32),
            pltpu.VMEM((tk, tq), F32),
        ],
    )
    return pl.pallas_call(
        functools.partial(_attn_kernel, dh=dh),
        out_shape=jax.ShapeDtypeStruct(qt.shape, BF16),
        grid_spec=grid_spec,
        compiler_params=_params(2),
        name="sb_attn",
    )(*_attn_tables(nq), qt, k, vt)


def _lru_kernel(x_ref, mod_ref, g_ref, win_ref, cw_ref, cb_ref, wr_ref, br_ref, wi_ref, bi_ref,
                lam_ref, wout_ref, o_ref, xb_scr, hs_scr, ga_scr, gb_scr, hin_scr, h_scr, *, sub, n_sub):
    ts = x_ref.shape[0]
    dr = cb_ref.shape[-1]
    n_taps = cw_ref.shape[0]
    grp = 8
    pad = grp
    rows_sub = ts // n_sub
    n_slabs, n_grp = dr // LANES, rows_sub // grp
    first = pad - (n_taps - 1)
    assert wr_ref.shape[-1] == LANES and first >= 0

    @pl.when(pl.program_id(1) == 0)
    def _():
        xb_scr[:, 0:pad, :] = jnp.zeros((n_slabs, pad, LANES), F32)
        h_scr[...] = jnp.zeros_like(h_scr)

    x = x_ref[...]
    shift, scale, gate = _mod_rows(mod_ref, sub)
    h = _norm_mod(x, g_ref[sub:sub + 1, :], shift, scale).astype(BF16)
    nl = -lam_ref[...]
    softplus_nl = jnp.maximum(nl, 0.0) + jnp.log(1.0 + jnp.exp(-jnp.abs(nl)))
    gate_br = [None] * n_sub
    hs_sub = [None] * n_sub
    state = {"h": tuple(h_scr[:, n * LANES:(n + 1) * LANES] for n in range(n_slabs))}

    def x_proj(j):
        xb = jnp.dot(h[j * rows_sub:(j + 1) * rows_sub], win_ref[:, dr:2 * dr], preferred_element_type=F32)
        for n in range(n_slabs):
            xb_scr[n, pad + j * rows_sub:pad + (j + 1) * rows_sub, :] = xb[:, n * LANES:(n + 1) * LANES]

    def gate_proj(j):
        gate_br[j] = jnp.dot(h[j * rows_sub:(j + 1) * rows_sub], win_ref[:, 0:dr],
                             preferred_element_type=F32)

    def middle(j):
        base = j * rows_sub
        cum = []
        for n in range(n_slabs):
            lanes = slice(n * LANES, (n + 1) * LANES)
            views = [xb_scr[n, pl.ds(base + o, n_grp, stride=grp), :] for o in range(first, pad + grp)]
            xc_steps = []
            for s in range(grp):
                acc = cb_ref[:, lanes]
                for k in range(n_taps):
                    acc = acc + views[s + k] * cw_ref[k:k + 1, lanes]
                xc_steps.append(acc)
            xc = jnp.concatenate(xc_steps, axis=0)
            xcb = xc.astype(BF16)
            r = jax.nn.sigmoid(jnp.dot(xcb, wr_ref[n], preferred_element_type=F32) + br_ref[:, lanes])
            ig = jax.nn.sigmoid(jnp.dot(xcb, wi_ref[n], preferred_element_type=F32) + bi_ref[:, lanes])
            a = jnp.exp((-LRU_C) * r * softplus_nl[:, lanes])
            v = 1.0 - a * a
            b = (v * lax.rsqrt(jnp.maximum(v, TINY))) * (ig * xc)
            ca = cb = None
            per_step = []
            for s in range(grp):
                a_s, b_s = a[s * n_grp:(s + 1) * n_grp], b[s * n_grp:(s + 1) * n_grp]
                ca, cb = (a_s, b_s) if s == 0 else (a_s * ca, a_s * cb + b_s)
                per_step.append((ca, cb))
            ga_scr[j, n], gb_scr[j, n] = ca, cb
            cum.append(per_step)
        hcur = list(state["h"])
        for g in range(n_grp):
            for n in range(n_slabs):
                hin_scr[j, n, g:g + 1, :] = hcur[n]
                hcur[n] = ga_scr[j, n, g:g + 1, :] * hcur[n] + gb_scr[j, n, g:g + 1, :]
        state["h"] = tuple(hcur)
        for n in range(n_slabs):
            h_in = hin_scr[j, n]
            for s in range(grp):
                ca, cb = cum[n][s]
                hs_scr[n, pl.ds(base + s, n_grp, stride=grp), :] = ca * h_in + cb
        hs_sub[j] = jnp.concatenate([hs_scr[n, base:base + rows_sub, :] for n in range(n_slabs)], axis=1)

    def out_proj(j):
        rows = slice(j * rows_sub, (j + 1) * rows_sub)
        y = (jax.nn.gelu(gate_br[j]) * hs_sub[j]).astype(BF16)
        out = jnp.dot(y, wout_ref[...], preferred_element_type=F32)
        o_ref[rows, :] = x[rows] + (1.0 + gate) * out

    x_proj(0)
    for j in range(n_sub):
        if j + 1 < n_sub:
            x_proj(j + 1)
        gate_proj(j)
        if j >= 1:
            out_proj(j - 1)
        middle(j)
    out_proj(n_sub - 1)
    for n in range(n_slabs):
        h_scr[:, n * LANES:(n + 1) * LANES] = state["h"][n]
        xb_scr[n, 0:pad, :] = xb_scr[n, ts:ts + pad, :]


def _lru_call(x, mod_l, norm_g_l, w_in, conv_w, conv_b, w_r, b_r, w_i, b_i, lam, w_out, sub):
    b, s, d = x.shape
    dr = w_out.shape[0]
    nb, bw, _ = w_r.shape
    ts, n_sub = 512, 4
    n_grp = ts // n_sub // 8
    assert s % ts == 0 and dr == nb * bw and bw == LANES
    row = lambda v: v.reshape(1, dr)
    const2 = lambda bi, i: (0, 0)
    const3 = lambda bi, i: (0, 0, 0)
    return pl.pallas_call(
        functools.partial(_lru_kernel, sub=sub, n_sub=n_sub),
        out_shape=jax.ShapeDtypeStruct((b, s, d), F32),
        grid=(b, s // ts),
        in_specs=[
            pl.BlockSpec((None, ts, d), lambda bi, i: (bi, i, 0)),
            pl.BlockSpec((None, N_SUB * 3, d), lambda bi, i: (bi, 0, 0)),
            pl.BlockSpec((N_SUB, d), const2),
            _resident((d, 2 * dr), const2),
            pl.BlockSpec(conv_w.shape, const2),
            pl.BlockSpec((1, dr), const2),
            _resident((nb, bw, bw), const3),
            pl.BlockSpec((1, dr), const2),
            _resident((nb, bw, bw), const3),
            pl.BlockSpec((1, dr), const2),
            pl.BlockSpec((1, dr), const2),
            _resident((dr, d), const2),
        ],
        out_specs=pl.BlockSpec((None, ts, d), lambda bi, i: (bi, i, 0)),
        scratch_shapes=[
            pltpu.VMEM((dr // LANES, ts + 8, LANES), F32),
            pltpu.VMEM((dr // LANES, ts, LANES), F32),
            pltpu.VMEM((n_sub, dr // LANES, n_grp, LANES), F32),
            pltpu.VMEM((n_sub, dr // LANES, n_grp, LANES), F32),
            pltpu.VMEM((n_sub, dr // LANES, n_grp, LANES), F32),
            pltpu.VMEM((1, dr), F32),
        ],
        compiler_params=_params(2),
        name="rglru",
    )(x, mod_l, norm_g_l, w_in.astype(BF16), conv_w, row(conv_b), w_r.astype(BF16), row(b_r),
      w_i.astype(BF16), row(b_i), row(lam), w_out.astype(BF16))


def kernel(x, c, mod_w, mod_b, norm_g, ffn_w_gu, ffn_w_down, sb_w_qkv, sb_w_o, lru_w_in, lru_conv_w,
           lru_conv_b, lru_w_r, lru_b_r, lru_w_i, lru_b_i, lru_lambda, lru_w_out, final_norm_g):
    depth = mod_w.shape[0]
    b, s, d = x.shape
    mod = _mod_call(c, mod_w, mod_b).reshape(depth, b, N_SUB * 3, d)
    w_gu_all, w_down_all = ffn_w_gu.astype(BF16), ffn_w_down.astype(BF16)
    for layer in range(depth):
        mod_l, ng = mod[layer], norm_g[layer]
        final_g = final_norm_g if layer == depth - 1 else None
        j = layer // 2
        if layer % 2 == 0:
            x, qt, k, vt = _ffn_call(x, mod_l, ng, w_gu_all, w_down_all, layer, 0, sub=0, w_qkv=sb_w_qkv[j])
            ot = _attn_call(qt, k, vt)
            x = _ffn_call(x, mod_l, ng, w_gu_all, w_down_all, layer, 1, sub=2, final_g=final_g,
                          attn_out=ot, w_o=sb_w_o[j])
        else:
            x = _ffn_call(x, mod_l, ng, w_gu_all, w_down_all, layer, 0, sub=0)
            x = _lru_call(x, mod_l, ng, lru_w_in[j], lru_conv_w[j], lru_conv_b[j], lru_w_r[j],
                          lru_b_r[j], lru_w_i[j], lru_b_i[j], lru_lambda[j], lru_w_out[j], sub=1)
            x = _ffn_call(x, mod_l, ng, w_gu_all, w_down_all, layer, 1, sub=2, final_g=final_g)
    return x
```

```python
import functools

import jax
import jax.numpy as jnp
import numpy as np
from jax import lax
from jax.experimental import pallas as pl
from jax.experimental.pallas import tpu as pltpu

SB_HEADS = 16
LRU_C = 8.0
MACARON_W = 0.5
EPS = 1e-6
N_SUB = 3

LANES = 128
MXU_TILE = 256
VMEM_LIMIT_BYTES = 56 * 2 ** 20

ATTN_HEADS_PER_STEP = 8
NEG_BIG = -1e30
LOG2_E = 1.4426950408889634
TINY = 1e-30

F32 = jnp.float32
BF16 = jnp.bfloat16


def _params(n_grid_axes):
    return pltpu.CompilerParams(
        dimension_semantics=("arbitrary",) * n_grid_axes,
        vmem_limit_bytes=VMEM_LIMIT_BYTES,
    )


def _resident(block_shape, index_map):
    return pl.BlockSpec(block_shape, index_map, pipeline_mode=pl.Buffered(1))


def _norm_mod(x, g, shift, scale):
    inv = lax.rsqrt(jnp.mean(x * x, axis=-1, keepdims=True) + EPS)
    return (x * inv) * (g * (1.0 + scale)) + shift


def _mod_rows(mod_ref, sub):
    r = N_SUB * sub
    return mod_ref[r:r + 1, :], mod_ref[r + 1:r + 2, :], mod_ref[r + 2:r + 3, :]


def _mod_kernel(c_ref, w_ref, b_ref, o_ref):
    c = c_ref[...]
    ca = c * jax.nn.sigmoid(c)
    ca2 = jnp.concatenate([ca, ca], axis=0).astype(BF16)
    y = jnp.dot(ca2, w_ref[...].astype(BF16), preferred_element_type=F32)
    o_ref[...] = y[:c.shape[0]] + b_ref[...]


def _mod_call(c, mod_w, mod_b):
    depth, d, n = mod_w.shape
    b = c.shape[0]
    tn = 1024
    assert n % tn == 0
    return pl.pallas_call(
        _mod_kernel,
        out_shape=jax.ShapeDtypeStruct((depth, b, n), F32),
        grid=(depth, n // tn),
        in_specs=[
            pl.BlockSpec((b, d), lambda l, j: (0, 0)),
            pl.BlockSpec((None, d, tn), lambda l, j: (l, 0, j)),
            pl.BlockSpec((None, 1, tn), lambda l, j: (l, 0, j)),
        ],
        out_specs=pl.BlockSpec((None, b, tn), lambda l, j: (l, 0, j)),
        compiler_params=_params(2),
        name="mod",
    )(c, mod_w, mod_b.reshape(depth, 1, n))


def _ffn_kernel(x_ref, mod_ref, g_ref, wgu_ref, wd_ref, *rest, sub, final, mix_sub, pre_oproj, post_qkv,
                cast_next, scale_q):
    rest = list(rest)
    if pre_oproj:
        ot_ref, wo_ref = rest[:2]
        rest = rest[2:]
    if post_qkv:
        wqkv_ref = rest.pop(0)
    if final:
        fg_ref = rest.pop(0)
    if cast_next:
        ngu_ref, nwd_ref = rest[:2]
        rest = rest[2:]
        ngu_out_ref, nwd_out_ref = rest[-2:]
        rest = rest[:-2]
        ngu_out_ref[...] = ngu_ref[...].astype(BF16)
        nwd_out_ref[...] = nwd_ref[...].astype(BF16)
    o_ref = rest.pop(0)
    if pre_oproj:
        x = _oproj_residual(ot_ref, x_ref, mod_ref, wo_ref, mix_sub)
    else:
        x = x_ref[...]
    d_ff = wd_ref.shape[0]
    tf = MXU_TILE
    shift, scale, gate = _mod_rows(mod_ref, sub)
    h = _norm_mod(x, g_ref[sub:sub + 1, :], shift, scale).astype(BF16)
    acts = []
    for c in range(d_ff // tf):
        g = jnp.dot(h, wgu_ref[:, c * tf:(c + 1) * tf], preferred_element_type=F32)
        u = jnp.dot(h, wgu_ref[:, d_ff + c * tf:d_ff + (c + 1) * tf], preferred_element_type=F32)
        acts.append((g * jax.nn.sigmoid(g) * u).astype(BF16))
    y = jnp.dot(jnp.concatenate(acts, axis=1), wd_ref[...], preferred_element_type=F32)
    out = x + (MACARON_W * (1.0 + gate)) * y
    if final:
        inv = lax.rsqrt(jnp.mean(out * out, axis=-1, keepdims=True) + EPS)
        out = (out * inv) * fg_ref[...]
    o_ref[...] = out
    if post_qkv:
        _qkv_emit(out, mod_ref, g_ref, wqkv_ref, *rest, sub=mix_sub, scale_q=scale_q)


def _oproj_residual(ot_ref, x_ref, mod_ref, w_ref, sub):
    _, _, gate = _mod_rows(mod_ref, sub)
    tt = ot_ref.shape[-1]
    parts = []
    for j in range(ot_ref.shape[0]):
        y = lax.dot_general(ot_ref[j], w_ref[...], (((0,), (0,)), ((), ())),
                            preferred_element_type=F32)
        parts.append(x_ref[j * tt:(j + 1) * tt, :] + (1.0 + gate) * y)
    return jnp.concatenate(parts, axis=0)


def _qkv_emit(x, mod_ref, g_ref, w_ref, qt_ref, k_ref, vt_ref, *, sub, scale_q):
    d = x.shape[-1]
    shift, scale, _ = _mod_rows(mod_ref, sub)
    h = _norm_mod(x, g_ref[sub:sub + 1, :], shift, scale).astype(BF16)
    q = jnp.dot(h, w_ref[:, 0:d], preferred_element_type=F32) * scale_q
    k_ref[...] = jnp.dot(h, w_ref[:, d:2 * d], preferred_element_type=F32).astype(BF16)
    v = jnp.dot(h, w_ref[:, 2 * d:3 * d], preferred_element_type=F32)
    tt = vt_ref.shape[-1]
    for j in range(vt_ref.shape[0]):
        qt_ref[j] = q[j * tt:(j + 1) * tt, :].T.astype(BF16)
        vt_ref[j] = v[j * tt:(j + 1) * tt, :].T.astype(BF16)


def _ffn_call(x, mod_l, norm_g_l, w_gu, w_down, sub, final_g=None, attn_out=None, w_o=None, w_qkv=None,
              next_weights=None, mix_sub=1, tt=MXU_TILE):
    b, s, d = x.shape
    d_ff = w_down.shape[0]
    tm = 512
    assert s % tm == 0 and d_ff % MXU_TILE == 0 and tm % tt == 0
    final, pre_oproj, post_qkv = final_g is not None, attn_out is not None, w_qkv is not None
    cast_next = next_weights is not None
    n_steps = b * (s // tm)
    tile_spec = pl.BlockSpec((None, tm, d), lambda bi, i: (bi, i, 0))
    tiled_t_spec = pl.BlockSpec((None, tm // tt, d, tt), lambda bi, i: (bi, i, 0, 0))
    in_specs = [
        tile_spec,
        pl.BlockSpec((None, N_SUB * 3, d), lambda bi, i: (bi, 0, 0)),
        pl.BlockSpec((N_SUB, d), lambda bi, i: (0, 0)),
        _resident((d, 2 * d_ff), lambda bi, i: (0, 0)),
        _resident((d_ff, d), lambda bi, i: (0, 0)),
    ]
    args = [x, mod_l, norm_g_l, w_gu, w_down]
    if pre_oproj:
        in_specs += [tiled_t_spec, _resident((d, d), lambda bi, i: (0, 0))]
        args += [attn_out, w_o.astype(BF16)]
    if post_qkv:
        in_specs.append(_resident((d, 3 * d), lambda bi, i: (0, 0)))
        args.append(w_qkv.astype(BF16))
    if final:
        in_specs.append(pl.BlockSpec((1, d), lambda bi, i: (0, 0)))
        args.append(final_g.reshape(1, d))
    out_shape = [jax.ShapeDtypeStruct((b, s, d), F32)]
    out_specs = [tile_spec]
    if post_qkv:
        tiled_t = jax.ShapeDtypeStruct((b, s // tt, d, tt), BF16)
        out_shape += [tiled_t, jax.ShapeDtypeStruct((b, s, d), BF16), tiled_t]
        out_specs += [tiled_t_spec, tile_spec, tiled_t_spec]
    if cast_next:
        w_gu_f32, w_down_f32, nl_, nh_ = next_weights
        depth, halves = w_gu_f32.shape[:2]
        gu_rows, wd_rows = d * 2 * d_ff // LANES, d_ff * d // LANES
        assert gu_rows % (16 * n_steps) == 0 and wd_rows % (16 * n_steps) == 0
        flat = lambda bi, i: bi * (s // tm) + i
        for w, rows in ((w_gu_f32, gu_rows), (w_down_f32, wd_rows)):
            in_specs.append(pl.BlockSpec((None, None, rows // n_steps, LANES),
                                         lambda bi, i: (nl_, nh_, flat(bi, i), 0)))
            args.append(w.reshape(depth, halves, rows, LANES))
            out_shape.append(jax.ShapeDtypeStruct((rows, LANES), BF16))
            out_specs.append(pl.BlockSpec((rows // n_steps, LANES), lambda bi, i: (flat(bi, i), 0)))
    dh = d // SB_HEADS
    res = pl.pallas_call(
        functools.partial(_ffn_kernel, sub=sub, final=final, mix_sub=mix_sub, pre_oproj=pre_oproj,
                          post_qkv=post_qkv, cast_next=cast_next, scale_q=dh ** -0.5 * LOG2_E),
        out_shape=tuple(out_shape),
        grid=(b, s // tm),
        in_specs=in_specs,
        out_specs=tuple(out_specs),
        compiler_params=_params(2),
        name="ffn" + ("_oproj" if pre_oproj else "") + ("_qkv" if post_qkv else "") + ("_final" if final else ""),
    )(*args)
    res = list(res)
    if cast_next:
        res[-2] = res[-2].reshape(d, 2 * d_ff)
        res[-1] = res[-1].reshape(d_ff, d)
    return res[0] if len(res) == 1 else tuple(res)


def _attn_tables(n_qblocks):
    qb = [0, 0] + list(range(n_qblocks))
    kb = [0, 0] + list(range(n_qblocks))
    dg = [1, 1] + [1] * n_qblocks
    for i in range(n_qblocks):
        for t in range(1, i + 1):
            qb.append(i)
            kb.append(i - t)
            dg.append(0)
    n_back = 2 + len(qb) % 2
    qb, kb, dg = qb + [0] * n_back, kb + [0] * n_back, dg + [1] * n_back
    return tuple(jnp.asarray(np.asarray(v, np.int32)) for v in (qb, kb, dg))


def _attn_kernel(qb_tab, kb_tab, dg_tab, qt_ref, k_ref, vt_ref, ot_ref,
                 z_scr, lb_scr, later_scr, acc_scr, tail_scr, tail_at_scr, cap_scr, *, dh):
    tq = qt_ref.shape[-1]
    tk = vt_ref.shape[-1]
    hp = 2 * dh
    n_heads = qt_ref.shape[1] // dh
    n_iter = qb_tab.shape[0] - 2
    n_diag = qt_ref.shape[0]
    assert n_diag % 2 == 0 and n_iter % 2 == 0

    kr = lax.broadcasted_iota(jnp.int32, (tk, tq), 0)
    qc = lax.broadcasted_iota(jnp.int32, (tk, tq), 1)
    cap_scr[...] = jnp.where(kr < qc, jnp.inf, NEG_BIG).astype(F32)
    z_scr[1] = jnp.full(z_scr.shape[1:], NEG_BIG, F32)
    lb_scr[...] = jnp.full(lb_scr.shape, NEG_BIG, F32)
    later_scr[...] = jnp.zeros(later_scr.shape, F32)
    acc_scr[...] = jnp.zeros(acc_scr.shape, F32)
    tail_scr[...] = jnp.zeros(tail_scr.shape, F32)
    tail_at_scr[...] = jnp.zeros(tail_at_scr.shape, F32)

    jr = lax.broadcasted_iota(jnp.int32, (tk, tk), 0)
    jc = lax.broadcasted_iota(jnp.int32, (tk, tk), 1)
    later_mat = (jc > jr).astype(BF16)
    frow = lax.broadcasted_iota(jnp.int32, (hp, tq), 0)

    def stages(m, slot, diagonal):
        qb_s, kb_s = qb_tab[m + 2], kb_tab[m + 2]
        qb_m, dg_m = qb_tab[m + 1], dg_tab[m + 1]
        qb_o, kb_o = qb_tab[m], kb_tab[m]
        keep_tail = jnp.where(dg_m == 1, 0.0, 1.0)
        k_all = k_ref[pl.ds(pl.multiple_of(kb_s * tk, tk), tk), :]
        qt_all = qt_ref[qb_s]
        vt_all = vt_ref[kb_o]
        for hd in range(n_heads):
            p, sub = divmod(hd, 2)
            pair = slice(p * hp, (p + 1) * hp)
            w = jnp.exp2(lb_scr[hd] + later_scr[hd]).astype(BF16)
            pv = jnp.dot(vt_all[pair, :], w, preferred_element_type=F32)
            rows = slice(hd * dh, (hd + 1) * dh)
            acc_scr[qb_o, rows, :] += pv[sub * dh:(sub + 1) * dh, :] * jnp.exp2(tail_at_scr[hd])
            zz = z_scr[1 - slot, hd]
            log_beta = jnp.minimum(zz, 0.0) - jnp.log(1.0 + jnp.exp2(-jnp.abs(zz))) * LOG2_E
            log_keep = log_beta - zz
            lb_scr[hd] = log_beta
            later = jnp.dot(later_mat, log_keep.astype(BF16), preferred_element_type=F32)
            later_scr[hd] = later
            tail_at = tail_scr[qb_m, hd] * keep_tail
            tail_at_scr[hd] = tail_at
            tail_scr[qb_m, hd] = tail_at + later[0:1, :] + log_keep[0:1, :]
            qt = qt_all[pair, :]
            qt = jnp.where((frow >= sub * dh) & (frow < (sub + 1) * dh), qt, jnp.zeros_like(qt))
            z = jnp.dot(k_all[:, pair], qt, preferred_element_type=F32)
            z_scr[slot, hd] = jnp.minimum(z, cap_scr[...]) if diagonal else z

    def body(i, carry, diagonal):
        stages(2 * i, 0, diagonal)
        stages(2 * i + 1, 1, diagonal)
        return carry

    lax.fori_loop(0, n_diag // 2, functools.partial(body, diagonal=True), 0)
    lax.fori_loop(n_diag // 2, n_iter // 2, functools.partial(body, diagonal=False), 0)
    ot_ref[...] = acc_scr[...].astype(BF16)


def _attn_call(qt, k, vt):
    b, nq, d, tq = qt.shape
    tk = vt.shape[-1]
    s = k.shape[1]
    dh = d // SB_HEADS
    assert 2 * dh == LANES and tq == tk and nq * tq == s
    hg = ATTN_HEADS_PER_STEP * dh
    assert ATTN_HEADS_PER_STEP % 2 == 0 and d % hg == 0
    tiled_spec = pl.BlockSpec((None, nq, hg, tq), lambda bi, g, *_: (bi, 0, g, 0))
    grid_spec = pltpu.PrefetchScalarGridSpec(
        num_scalar_prefetch=3,
        grid=(b, d // hg),
        in_specs=[tiled_spec, pl.BlockSpec((None, s, hg), lambda bi, g, *_: (bi, 0, g)), tiled_spec],
        out_specs=tiled_spec,
        scratch_shapes=[
            pltpu.VMEM((2, ATTN_HEADS_PER_STEP, tk, tq), F32),
            pltpu.VMEM((ATTN_HEADS_PER_STEP, tk, tq), F32),
            pltpu.VMEM((ATTN_HEADS_PER_STEP, tk, tq), F32),
            pltpu.VMEM((nq, hg, tq), F32),
            pltpu.VMEM((nq, ATTN_HEADS_PER_STEP, 1, tq), F32),
            pltpu.VMEM((ATTN_HEADS_PER_STEP, 1, tq), F32),
            pltpu.VMEM((tk, tq), F32),
        ],
    )
    return pl.pallas_call(
        functools.partial(_attn_kernel, dh=dh),
        out_shape=jax.ShapeDtypeStruct(qt.shape, BF16),
        grid_spec=grid_spec,
        compiler_params=_params(2),
        name="sb_attn",
    )(*_attn_tables(nq), qt, k, vt)


def _lru_kernel(x_ref, mod_ref, g_ref, win_ref, cw_ref, cb_ref, wr_ref, br_ref, wi_ref, bi_ref,
                lam_ref, wout_ref, o_ref, xb_scr, hs_scr, ga_scr, gb_scr, hin_scr, h_scr, *, sub, n_sub):
    ts = x_ref.shape[0]
    dr = cb_ref.shape[-1]
    n_taps = cw_ref.shape[0]
    grp = 8
    pad = grp
    rows_sub = ts // n_sub
    n_slabs, n_grp = dr // LANES, rows_sub // grp
    first = pad - (n_taps - 1)
    assert wr_ref.shape[-1] == LANES and first >= 0

    @pl.when(pl.program_id(1) == 0)
    def _():
        xb_scr[:, 0:pad, :] = jnp.zeros((n_slabs, pad, LANES), F32)
        h_scr[...] = jnp.zeros_like(h_scr)

    x = x_ref[...]
    shift, scale, gate = _mod_rows(mod_ref, sub)
    h = _norm_mod(x, g_ref[sub:sub + 1, :], shift, scale).astype(BF16)
    nl = -lam_ref[...]
    softplus_nl = jnp.maximum(nl, 0.0) + jnp.log(1.0 + jnp.exp(-jnp.abs(nl)))
    gate_br = [None] * n_sub
    hs_sub = [None] * n_sub
    state = {"h": tuple(h_scr[:, n * LANES:(n + 1) * LANES] for n in range(n_slabs))}

    def x_proj(j):
        xb = jnp.dot(h[j * rows_sub:(j + 1) * rows_sub], win_ref[:, dr:2 * dr], preferred_element_type=F32)
        for n in range(n_slabs):
            xb_scr[n, pad + j * rows_sub:pad + (j + 1) * rows_sub, :] = xb[:, n * LANES:(n + 1) * LANES]

    def gate_proj(j):
        gate_br[j] = jnp.dot(h[j * rows_sub:(j + 1) * rows_sub], win_ref[:, 0:dr],
                             preferred_element_type=F32)

    def middle(j):
        base = j * rows_sub
        cum = []
        for n in range(n_slabs):
            lanes = slice(n * LANES, (n + 1) * LANES)
            views = [xb_scr[n, pl.ds(base + o, n_grp, stride=grp), :] for o in range(first, pad + grp)]
            xc_steps = []
            for s in range(grp):
                acc = cb_ref[:, lanes]
                for k in range(n_taps):
                    acc = acc + views[s + k] * cw_ref[k:k + 1, lanes]
                xc_steps.append(acc)
            xc = jnp.concatenate(xc_steps, axis=0)
            xcb = xc.astype(BF16)
            r = jax.nn.sigmoid(jnp.dot(xcb, wr_ref[n], preferred_element_type=F32) + br_ref[:, lanes])
            ig = jax.nn.sigmoid(jnp.dot(xcb, wi_ref[n], preferred_element_type=F32) + bi_ref[:, lanes])
            a = jnp.exp((-LRU_C) * r * softplus_nl[:, lanes])
            v = 1.0 - a * a
            b = (v * lax.rsqrt(jnp.maximum(v, TINY))) * (ig * xc)
            ca = cb = None
            per_step = []
            for s in range(grp):
                a_s, b_s = a[s * n_grp:(s + 1) * n_grp], b[s * n_grp:(s + 1) * n_grp]
                ca, cb = (a_s, b_s) if s == 0 else (a_s * ca, a_s * cb + b_s)
                per_step.append((ca, cb))
            ga_scr[j, n], gb_scr[j, n] = ca, cb
            cum.append(per_step)
        hcur = list(state["h"])
        for g in range(n_grp):
            for n in range(n_slabs):
                hin_scr[j, n, g:g + 1, :] = hcur[n]
                hcur[n] = ga_scr[j, n, g:g + 1, :] * hcur[n] + gb_scr[j, n, g:g + 1, :]
        state["h"] = tuple(hcur)
        for n in range(n_slabs):
            h_in = hin_scr[j, n]
            for s in range(grp):
                ca, cb = cum[n][s]
                hs_scr[n, pl.ds(base + s, n_grp, stride=grp), :] = ca * h_in + cb
        hs_sub[j] = jnp.concatenate([hs_scr[n, base:base + rows_sub, :] for n in range(n_slabs)], axis=1)

    def out_proj(j):
        rows = slice(j * rows_sub, (j + 1) * rows_sub)
        y = (jax.nn.gelu(gate_br[j]) * hs_sub[j]).astype(BF16)
        out = jnp.dot(y, wout_ref[...], preferred_element_type=F32)
        o_ref[rows, :] = x[rows] + (1.0 + gate) * out

    x_proj(0)
    for j in range(n_sub):
        if j + 1 < n_sub:
            x_proj(j + 1)
        gate_proj(j)
        if j >= 1:
            out_proj(j - 1)
        middle(j)
    out_proj(n_sub - 1)
    for n in range(n_slabs):
        h_scr[:, n * LANES:(n + 1) * LANES] = state["h"][n]
        xb_scr[n, 0:pad, :] = xb_scr[n, ts:ts + pad, :]


def _lru_call(x, mod_l, norm_g_l, w_in, conv_w, conv_b, w_r, b_r, w_i, b_i, lam, w_out, sub):
    b, s, d = x.shape
    dr = w_out.shape[0]
    nb, bw, _ = w_r.shape
    ts, n_sub = 512, 4
    n_grp = ts // n_sub // 8
    assert s % ts == 0 and dr == nb * bw and bw == LANES
    row = lambda v: v.reshape(1, dr)
    const2 = lambda bi, i: (0, 0)
    const3 = lambda bi, i: (0, 0, 0)
    return pl.pallas_call(
        functools.partial(_lru_kernel, sub=sub, n_sub=n_sub),
        out_shape=jax.ShapeDtypeStruct((b, s, d), F32),
        grid=(b, s // ts),
        in_specs=[
            pl.BlockSpec((None, ts, d), lambda bi, i: (bi, i, 0)),
            pl.BlockSpec((None, N_SUB * 3, d), lambda bi, i: (bi, 0, 0)),
            pl.BlockSpec((N_SUB, d), const2),
            _resident((d, 2 * dr), const2),
            pl.BlockSpec(conv_w.shape, const2),
            pl.BlockSpec((1, dr), const2),
            _resident((nb, bw, bw), const3),
            pl.BlockSpec((1, dr), const2),
            _resident((nb, bw, bw), const3),
            pl.BlockSpec((1, dr), const2),
            pl.BlockSpec((1, dr), const2),
            _resident((dr, d), const2),
        ],
        out_specs=pl.BlockSpec((None, ts, d), lambda bi, i: (bi, i, 0)),
        scratch_shapes=[
            pltpu.VMEM((dr // LANES, ts + 8, LANES), F32),
            pltpu.VMEM((dr // LANES, ts, LANES), F32),
            pltpu.VMEM((n_sub, dr // LANES, n_grp, LANES), F32),
            pltpu.VMEM((n_sub, dr // LANES, n_grp, LANES), F32),
            pltpu.VMEM((n_sub, dr // LANES, n_grp, LANES), F32),
            pltpu.VMEM((1, dr), F32),
        ],
        compiler_params=_params(2),
        name="rglru",
    )(x, mod_l, norm_g_l, w_in.astype(BF16), conv_w, row(conv_b), w_r.astype(BF16), row(b_r),
      w_i.astype(BF16), row(b_i), row(lam), w_out.astype(BF16))


def kernel(x, c, mod_w, mod_b, norm_g, ffn_w_gu, ffn_w_down, sb_w_qkv, sb_w_o, lru_w_in, lru_conv_w,
           lru_conv_b, lru_w_r, lru_b_r, lru_w_i, lru_b_i, lru_lambda, lru_w_out, final_norm_g):
    depth = mod_w.shape[0]
    b, s, d = x.shape
    mod = _mod_call(c, mod_w, mod_b).reshape(depth, b, N_SUB * 3, d)
    w_gu, w_down = ffn_w_gu[0, 0].astype(BF16), ffn_w_down[0, 0].astype(BF16)
    for layer in range(depth):
        mod_l, ng = mod[layer], norm_g[layer]
        last = layer == depth - 1
        nxt_b = (ffn_w_gu, ffn_w_down, layer, 1)
        nxt_a = None if last else (ffn_w_gu, ffn_w_down, layer + 1, 0)
        j = layer // 2
        if layer % 2 == 0:
            x, qt, k, vt, w_gu, w_down = _ffn_call(x, mod_l, ng, w_gu, w_down, sub=0, w_qkv=sb_w_qkv[j],
                                                   next_weights=nxt_b)
            ot = _attn_call(qt, k, vt)
            res = _ffn_call(x, mod_l, ng, w_gu, w_down, sub=2, final_g=final_norm_g if last else None,
                            attn_out=ot, w_o=sb_w_o[j], next_weights=nxt_a)
        else:
            x, w_gu, w_down = _ffn_call(x, mod_l, ng, w_gu, w_down, sub=0, next_weights=nxt_b)
            x = _lru_call(x, mod_l, ng, lru_w_in[j], lru_conv_w[j], lru_conv_b[j], lru_w_r[j],
                          lru_b_r[j], lru_w_i[j], lru_b_i[j], lru_lambda[j], lru_w_out[j], sub=1)
            res = _ffn_call(x, mod_l, ng, w_gu, w_down, sub=2, final_g=final_norm_g if last else None,
                            next_weights=nxt_a)
        x, w_gu, w_down = (res, None, None) if last else res
    return x
```

```python
import functools

import jax
import jax.numpy as jnp
import numpy as np
from jax import lax
from jax.experimental import pallas as pl
from jax.experimental.pallas import tpu as pltpu

SB_HEADS = 16
LRU_C = 8.0
MACARON_W = 0.5
EPS = 1e-6
N_SUB = 3

LANES = 128
MXU_TILE = 256
VMEM_LIMIT_BYTES = 56 * 2 ** 20

ATTN_HEADS_PER_STEP = 8
NEG_BIG = -1e30
LOG2_E = 1.4426950408889634
TINY = 1e-30

F32 = jnp.float32
BF16 = jnp.bfloat16


def _params(n_grid_axes):
    return pltpu.CompilerParams(
        dimension_semantics=("arbitrary",) * n_grid_axes,
        vmem_limit_bytes=VMEM_LIMIT_BYTES,
    )


def _resident(block_shape, index_map):
    return pl.BlockSpec(block_shape, index_map, pipeline_mode=pl.Buffered(1))


def _norm_mod(x, g, shift, scale):
    inv = lax.rsqrt(jnp.mean(x * x, axis=-1, keepdims=True) + EPS)
    return (x * inv) * (g * (1.0 + scale)) + shift


def _mod_rows(mod_ref, sub):
    r = N_SUB * sub
    return mod_ref[r:r + 1, :], mod_ref[r + 1:r + 2, :], mod_ref[r + 2:r + 3, :]


def _mod_kernel(c_ref, w_ref, b_ref, o_ref):
    c = c_ref[...]
    ca = c * jax.nn.sigmoid(c)
    ca2 = jnp.concatenate([ca, ca], axis=0).astype(BF16)
    y = jnp.dot(ca2, w_ref[...].astype(BF16), preferred_element_type=F32)
    o_ref[...] = y[:c.shape[0]] + b_ref[...]


def _mod_call(c, mod_w, mod_b):
    depth, d, n = mod_w.shape
    b = c.shape[0]
    tn = 1024
    assert n % tn == 0
    return pl.pallas_call(
        _mod_kernel,
        out_shape=jax.ShapeDtypeStruct((depth, b, n), F32),
        grid=(depth, n // tn),
        in_specs=[
            pl.BlockSpec((b, d), lambda l, j: (0, 0)),
            pl.BlockSpec((None, d, tn), lambda l, j: (l, 0, j)),
            pl.BlockSpec((None, 1, tn), lambda l, j: (l, 0, j)),
        ],
        out_specs=pl.BlockSpec((None, b, tn), lambda l, j: (l, 0, j)),
        compiler_params=_params(2),
        name="mod",
    )(c, mod_w, mod_b.reshape(depth, 1, n))


def _ffn_kernel(x_ref, mod_ref, g_ref, wgu_ref, wd_ref, *rest, sub, final, mix_sub, pre_oproj, post_qkv,
                cast_next, scale_q):
    rest = list(rest)
    if pre_oproj:
        ot_ref, wo_ref = rest[:2]
        rest = rest[2:]
    if post_qkv:
        wqkv_ref = rest.pop(0)
    if final:
        fg_ref = rest.pop(0)
    if cast_next:
        ngu_ref, nwd_ref = rest[:2]
        rest = rest[2:]
        ngu_out_ref, nwd_out_ref = rest[-2:]
        rest = rest[:-2]
        ngu_out_ref[...] = ngu_ref[...].astype(BF16)
        nwd_out_ref[...] = nwd_ref[...].astype(BF16)
    o_ref = rest.pop(0)
    if pre_oproj:
        x = _oproj_residual(ot_ref, x_ref, mod_ref, wo_ref, mix_sub)
    else:
        x = x_ref[...]
    d_ff = wd_ref.shape[0]
    tf = MXU_TILE
    shift, scale, gate = _mod_rows(mod_ref, sub)
    h = _norm_mod(x, g_ref[sub:sub + 1, :], shift, scale).astype(BF16)
    acts = []
    for c in range(d_ff // tf):
        g = jnp.dot(h, wgu_ref[:, c * tf:(c + 1) * tf], preferred_element_type=F32)
        u = jnp.dot(h, wgu_ref[:, d_ff + c * tf:d_ff + (c + 1) * tf], preferred_element_type=F32)
        acts.append((g * jax.nn.sigmoid(g) * u).astype(BF16))
    y = jnp.dot(jnp.concatenate(acts, axis=1), wd_ref[...], preferred_element_type=F32)
    out = x + (MACARON_W * (1.0 + gate)) * y
    if final:
        inv = lax.rsqrt(jnp.mean(out * out, axis=-1, keepdims=True) + EPS)
        out = (out * inv) * fg_ref[...]
    o_ref[...] = out
    if post_qkv:
        _qkv_emit(out, mod_ref, g_ref, wqkv_ref, *rest, sub=mix_sub, scale_q=scale_q)


def _oproj_residual(ot_ref, x_ref, mod_ref, w_ref, sub):
    _, _, gate = _mod_rows(mod_ref, sub)
    tt = ot_ref.shape[-1]
    parts = []
    for j in range(ot_ref.shape[0]):
        y = lax.dot_general(ot_ref[j], w_ref[...], (((0,), (0,)), ((), ())),
                            preferred_element_type=F32)
        parts.append(x_ref[j * tt:(j + 1) * tt, :] + (1.0 + gate) * y)
    return jnp.concatenate(parts, axis=0)


def _qkv_emit(x, mod_ref, g_ref, w_ref, qt_ref, k_ref, vt_ref, *, sub, scale_q):
    d = x.shape[-1]
    shift, scale, _ = _mod_rows(mod_ref, sub)
    h = _norm_mod(x, g_ref[sub:sub + 1, :], shift, scale).astype(BF16)
    q = jnp.dot(h, w_ref[:, 0:d], preferred_element_type=F32) * scale_q
    k_ref[...] = jnp.dot(h, w_ref[:, d:2 * d], preferred_element_type=F32).astype(BF16)
    v = jnp.dot(h, w_ref[:, 2 * d:3 * d], preferred_element_type=F32)
    tt = vt_ref.shape[-1]
    for j in range(vt_ref.shape[0]):
        qt_ref[j] = q[j * tt:(j + 1) * tt, :].T.astype(BF16)
        vt_ref[j] = v[j * tt:(j + 1) * tt, :].T.astype(BF16)


def _ffn_call(x, mod_l, norm_g_l, w_gu, w_down, sub, final_g=None, attn_out=None, w_o=None, w_qkv=None,
              next_weights=None, mix_sub=1, tt=MXU_TILE):
    b, s, d = x.shape
    d_ff = w_down.shape[0]
    tm = 512
    assert s % tm == 0 and d_ff % MXU_TILE == 0 and tm % tt == 0
    final, pre_oproj, post_qkv = final_g is not None, attn_out is not None, w_qkv is not None
    cast_next = next_weights is not None
    n_steps = b * (s // tm)
    tile_spec = pl.BlockSpec((None, tm, d), lambda bi, i: (bi, i, 0))
    tiled_t_spec = pl.BlockSpec((None, tm // tt, d, tt), lambda bi, i: (bi, i, 0, 0))
    in_specs = [
        tile_spec,
        pl.BlockSpec((None, N_SUB * 3, d), lambda bi, i: (bi, 0, 0)),
        pl.BlockSpec((N_SUB, d), lambda bi, i: (0, 0)),
        _resident((d, 2 * d_ff), lambda bi, i: (0, 0)),
        _resident((d_ff, d), lambda bi, i: (0, 0)),
    ]
    args = [x, mod_l, norm_g_l, w_gu, w_down]
    if pre_oproj:
        in_specs += [tiled_t_spec, _resident((d, d), lambda bi, i: (0, 0))]
        args += [attn_out, w_o.astype(BF16)]
    if post_qkv:
        in_specs.append(_resident((d, 3 * d), lambda bi, i: (0, 0)))
        args.append(w_qkv.astype(BF16))
    if final:
        in_specs.append(pl.BlockSpec((1, d), lambda bi, i: (0, 0)))
        args.append(final_g.reshape(1, d))
    out_shape = [jax.ShapeDtypeStruct((b, s, d), F32)]
    out_specs = [tile_spec]
    if post_qkv:
        tiled_t = jax.ShapeDtypeStruct((b, s // tt, d, tt), BF16)
        out_shape += [tiled_t, jax.ShapeDtypeStruct((b, s, d), BF16), tiled_t]
        out_specs += [tiled_t_spec, tile_spec, tiled_t_spec]
    if cast_next:
        w_gu_f32, w_down_f32, nl_, nh_ = next_weights
        flat = lambda bi, i: bi * (s // tm) + i
        for w in (w_gu_f32, w_down_f32):
            rows, cols = w.shape[-2:]
            k = next(k for k in (1, 2, 4, 8) if n_steps % k == 0 and (rows * k) % (16 * n_steps) == 0)
            blk = rows * k // n_steps
            in_specs.append(pl.BlockSpec((None, None, blk, cols),
                                         lambda bi, i, k=k: (nl_, nh_, flat(bi, i) // k, 0)))
            args.append(w)
            out_shape.append(jax.ShapeDtypeStruct((rows, cols), BF16))
            out_specs.append(pl.BlockSpec((blk, cols), lambda bi, i, k=k: (flat(bi, i) // k, 0)))
    dh = d // SB_HEADS
    res = pl.pallas_call(
        functools.partial(_ffn_kernel, sub=sub, final=final, mix_sub=mix_sub, pre_oproj=pre_oproj,
                          post_qkv=post_qkv, cast_next=cast_next, scale_q=dh ** -0.5 * LOG2_E),
        out_shape=tuple(out_shape),
        grid=(b, s // tm),
        in_specs=in_specs,
        out_specs=tuple(out_specs),
        compiler_params=_params(2),
        name="ffn" + ("_oproj" if pre_oproj else "") + ("_qkv" if post_qkv else "") + ("_final" if final else ""),
    )(*args)
    return res[0] if len(res) == 1 else tuple(res)


def _attn_tables(n_qblocks):
    qb = [0, 0] + list(range(n_qblocks))
    kb = [0, 0] + list(range(n_qblocks))
    dg = [1, 1] + [1] * n_qblocks
    for i in range(n_qblocks):
        for t in range(1, i + 1):
            qb.append(i)
            kb.append(i - t)
            dg.append(0)
    n_back = 2 + len(qb) % 2
    qb, kb, dg = qb + [0] * n_back, kb + [0] * n_back, dg + [1] * n_back
    return tuple(jnp.asarray(np.asarray(v, np.int32)) for v in (qb, kb, dg))


def _attn_kernel(qb_tab, kb_tab, dg_tab, qt_ref, k_ref, vt_ref, ot_ref,
                 z_scr, lb_scr, later_scr, acc_scr, tail_scr, tail_at_scr, cap_scr, *, dh):
    tq = qt_ref.shape[-1]
    tk = vt_ref.shape[-1]
    hp = 2 * dh
    n_heads = qt_ref.shape[1] // dh
    n_iter = qb_tab.shape[0] - 2
    n_diag = qt_ref.shape[0]
    assert n_diag % 2 == 0 and n_iter % 2 == 0

    kr = lax.broadcasted_iota(jnp.int32, (tk, tq), 0)
    qc = lax.broadcasted_iota(jnp.int32, (tk, tq), 1)
    cap_scr[...] = jnp.where(kr < qc, jnp.inf, NEG_BIG).astype(F32)
    z_scr[1] = jnp.full(z_scr.shape[1:], NEG_BIG, F32)
    lb_scr[...] = jnp.full(lb_scr.shape, NEG_BIG, F32)
    later_scr[...] = jnp.zeros(later_scr.shape, F32)
    acc_scr[...] = jnp.zeros(acc_scr.shape, F32)
    tail_scr[...] = jnp.zeros(tail_scr.shape, F32)
    tail_at_scr[...] = jnp.zeros(tail_at_scr.shape, F32)

    jr = lax.broadcasted_iota(jnp.int32, (tk, tk), 0)
    jc = lax.broadcasted_iota(jnp.int32, (tk, tk), 1)
    later_mat = (jc > jr).astype(BF16)
    frow = lax.broadcasted_iota(jnp.int32, (hp, tq), 0)

    def stages(m, slot, diagonal):
        qb_s, kb_s = qb_tab[m + 2], kb_tab[m + 2]
        qb_m, dg_m = qb_tab[m + 1], dg_tab[m + 1]
        qb_o, kb_o = qb_tab[m], kb_tab[m]
        keep_tail = jnp.where(dg_m == 1, 0.0, 1.0)
        k_all = k_ref[pl.ds(pl.multiple_of(kb_s * tk, tk), tk), :]
        qt_all = qt_ref[qb_s]
        vt_all = vt_ref[kb_o]
        for hd in range(n_heads):
            p, sub = divmod(hd, 2)
            pair = slice(p * hp, (p + 1) * hp)
            w = jnp.exp2(lb_scr[hd] + later_scr[hd]).astype(BF16)
            pv = jnp.dot(vt_all[pair, :], w, preferred_element_type=F32)
            rows = slice(hd * dh, (hd + 1) * dh)
            acc_scr[qb_o, rows, :] += pv[sub * dh:(sub + 1) * dh, :] * jnp.exp2(tail_at_scr[hd])
            zz = z_scr[1 - slot, hd]
            log_beta = jnp.minimum(zz, 0.0) - jnp.log(1.0 + jnp.exp2(-jnp.abs(zz))) * LOG2_E
            log_keep = log_beta - zz
            lb_scr[hd] = log_beta
            later = jnp.dot(later_mat, log_keep.astype(BF16), preferred_element_type=F32)
            later_scr[hd] = later
            tail_at = tail_scr[qb_m, hd] * keep_tail
            tail_at_scr[hd] = tail_at
            tail_scr[qb_m, hd] = tail_at + later[0:1, :] + log_keep[0:1, :]
            qt = qt_all[pair, :]
            qt = jnp.where((frow >= sub * dh) & (frow < (sub + 1) * dh), qt, jnp.zeros_like(qt))
            z = jnp.dot(k_all[:, pair], qt, preferred_element_type=F32)
            z_scr[slot, hd] = jnp.minimum(z, cap_scr[...]) if diagonal else z

    def body(i, carry, diagonal):
        stages(2 * i, 0, diagonal)
        stages(2 * i + 1, 1, diagonal)
        return carry

    lax.fori_loop(0, n_diag // 2, functools.partial(body, diagonal=True), 0)
    lax.fori_loop(n_diag // 2, n_iter // 2, functools.partial(body, diagonal=False), 0)
    ot_ref[...] = acc_scr[...].astype(BF16)


def _attn_call(qt, k, vt):
    b, nq, d, tq = qt.shape
    tk = vt.shape[-1]
    s = k.shape[1]
    dh = d // SB_HEADS
    assert 2 * dh == LANES and tq == tk and nq * tq == s
    hg = ATTN_HEADS_PER_STEP * dh
    assert ATTN_HEADS_PER_STEP % 2 == 0 and d % hg == 0
    tiled_spec = pl.BlockSpec((None, nq, hg, tq), lambda bi, g, *_: (bi, 0, g, 0))
    grid_spec = pltpu.PrefetchScalarGridSpec(
        num_scalar_prefetch=3,
        grid=(b, d // hg),
        in_specs=[tiled_spec, pl.BlockSpec((None, s, hg), lambda bi, g, *_: (bi, 0, g)), tiled_spec],
        out_specs=tiled_spec,
        scratch_shapes=[
            pltpu.VMEM((2, ATTN_HEADS_PER_STEP, tk, tq), F32),
            pltpu.VMEM((ATTN_HEADS_PER_STEP, tk, tq), F32),
            pltpu.VMEM((ATTN_HEADS_PER_STEP, tk, tq), F32),
            pltpu.VMEM((nq, hg, tq), F32),
            pltpu.VMEM((nq, ATTN_HEADS_PER_STEP, 1, tq), F32),
            pltpu.VMEM((ATTN_HEADS_PER_STEP, 1, tq), F32),
            pltpu.VMEM((tk, tq), F32),
        ],
    )
    return pl.pallas_call(
        functools.partial(_attn_kernel, dh=dh),
        out_shape=jax.ShapeDtypeStruct(qt.shape, BF16),
        grid_spec=grid_spec,
        compiler_params=_params(2),
        name="sb_attn",
    )(*_attn_tables(nq), qt, k, vt)


def _lru_kernel(x_ref, mod_ref, g_ref, win_ref, cw_ref, cb_ref, wr_ref, br_ref, wi_ref, bi_ref,
                lam_ref, wout_ref, o_ref, xb_scr, hs_scr, ga_scr, gb_scr, hin_scr, h_scr, *, sub, n_sub):
    ts = x_ref.shape[0]
    dr = cb_ref.shape[-1]
    n_taps = cw_ref.shape[0]
    grp = 8
    pad = grp
    rows_sub = ts // n_sub
    n_slabs, n_grp = dr // LANES, rows_sub // grp
    first = pad - (n_taps - 1)
    assert wr_ref.shape[-1] == LANES and first >= 0

    @pl.when(pl.program_id(1) == 0)
    def _():
        xb_scr[:, 0:pad, :] = jnp.zeros((n_slabs, pad, LANES), F32)
        h_scr[...] = jnp.zeros_like(h_scr)

    x = x_ref[...]
    shift, scale, gate = _mod_rows(mod_ref, sub)
    h = _norm_mod(x, g_ref[sub:sub + 1, :], shift, scale).astype(BF16)
    nl = -lam_ref[...]
    softplus_nl = jnp.maximum(nl, 0.0) + jnp.log(1.0 + jnp.exp(-jnp.abs(nl)))
    gate_br = [None] * n_sub
    hs_sub = [None] * n_sub
    state = {"h": tuple(h_scr[:, n * LANES:(n + 1) * LANES] for n in range(n_slabs))}

    def x_proj(j):
        xb = jnp.dot(h[j * rows_sub:(j + 1) * rows_sub], win_ref[:, dr:2 * dr], preferred_element_type=F32)
        for n in range(n_slabs):
            xb_scr[n, pad + j * rows_sub:pad + (j + 1) * rows_sub, :] = xb[:, n * LANES:(n + 1) * LANES]

    def gate_proj(j):
        gate_br[j] = jnp.dot(h[j * rows_sub:(j + 1) * rows_sub], win_ref[:, 0:dr],
                             preferred_element_type=F32)

    def middle(j):
        base = j * rows_sub
        cum = []
        for n in range(n_slabs):
            lanes = slice(n * LANES, (n + 1) * LANES)
            views = [xb_scr[n, pl.ds(base + o, n_grp, stride=grp), :] for o in range(first, pad + grp)]
            xc_steps = []
            for s in range(grp):
                acc = cb_ref[:, lanes]
                for k in range(n_taps):
                    acc = acc + views[s + k] * cw_ref[k:k + 1, lanes]
                xc_steps.append(acc)
            xc = jnp.concatenate(xc_steps, axis=0)
            xcb = xc.astype(BF16)
            r = jax.nn.sigmoid(jnp.dot(xcb, wr_ref[n], preferred_element_type=F32) + br_ref[:, lanes])
            ig = jax.nn.sigmoid(jnp.dot(xcb, wi_ref[n], preferred_element_type=F32) + bi_ref[:, lanes])
            a = jnp.exp((-LRU_C) * r * softplus_nl[:, lanes])
            v = 1.0 - a * a
            b = (v * lax.rsqrt(jnp.maximum(v, TINY))) * (ig * xc)
            ca = cb = None
            per_step = []
            for s in range(grp):
                a_s, b_s = a[s * n_grp:(s + 1) * n_grp], b[s * n_grp:(s + 1) * n_grp]
                ca, cb = (a_s, b_s) if s == 0 else (a_s * ca, a_s * cb + b_s)
                per_step.append((ca, cb))
            ga_scr[j, n], gb_scr[j, n] = ca, cb
            cum.append(per_step)
        hcur = list(state["h"])
        for g in range(n_grp):
            for n in range(n_slabs):
                hin_scr[j, n, g:g + 1, :] = hcur[n]
                hcur[n] = ga_scr[j, n, g:g + 1, :] * hcur[n] + gb_scr[j, n, g:g + 1, :]
        state["h"] = tuple(hcur)
        for n in range(n_slabs):
            h_in = hin_scr[j, n]
            for s in range(grp):
                ca, cb = cum[n][s]
                hs_scr[n, pl.ds(base + s, n_grp, stride=grp), :] = ca * h_in + cb
        hs_sub[j] = jnp.concatenate([hs_scr[n, base:base + rows_sub, :] for n in range(n_slabs)], axis=1)

    def out_proj(j):
        rows = slice(j * rows_sub, (j + 1) * rows_sub)
        y = (jax.nn.gelu(gate_br[j]) * hs_sub[j]).astype(BF16)
        out = jnp.dot(y, wout_ref[...], preferred_element_type=F32)
        o_ref[rows, :] = x[rows] + (1.0 + gate) * out

    x_proj(0)
    for j in range(n_sub):
        if j + 1 < n_sub:
            x_proj(j + 1)
        gate_proj(j)
        if j >= 1:
            out_proj(j - 1)
        middle(j)
    out_proj(n_sub - 1)
    for n in range(n_slabs):
        h_scr[:, n * LANES:(n + 1) * LANES] = state["h"][n]
        xb_scr[n, 0:pad, :] = xb_scr[n, ts:ts + pad, :]


def _lru_call(x, mod_l, norm_g_l, w_in, conv_w, conv_b, w_r, b_r, w_i, b_i, lam, w_out, sub):
    b, s, d = x.shape
    dr = w_out.shape[0]
    nb, bw, _ = w_r.shape
    ts, n_sub = 512, 4
    n_grp = ts // n_sub // 8
    assert s % ts == 0 and dr == nb * bw and bw == LANES
    row = lambda v: v.reshape(1, dr)
    const2 = lambda bi, i: (0, 0)
    const3 = lambda bi, i: (0, 0, 0)
    return pl.pallas_call(
        functools.partial(_lru_kernel, sub=sub, n_sub=n_sub),
        out_shape=jax.ShapeDtypeStruct((b, s, d), F32),
        grid=(b, s // ts),
        in_specs=[
            pl.BlockSpec((None, ts, d), lambda bi, i: (bi, i, 0)),
            pl.BlockSpec((None, N_SUB * 3, d), lambda bi, i: (bi, 0, 0)),
            pl.BlockSpec((N_SUB, d), const2),
            _resident((d, 2 * dr), const2),
            pl.BlockSpec(conv_w.shape, const2),
            pl.BlockSpec((1, dr), const2),
            _resident((nb, bw, bw), const3),
            pl.BlockSpec((1, dr), const2),
            _resident((nb, bw, bw), const3),
            pl.BlockSpec((1, dr), const2),
            pl.BlockSpec((1, dr), const2),
            _resident((dr, d), const2),
        ],
        out_specs=pl.BlockSpec((None, ts, d), lambda bi, i: (bi, i, 0)),
        scratch_shapes=[
            pltpu.VMEM((dr // LANES, ts + 8, LANES), F32),
            pltpu.VMEM((dr // LANES, ts, LANES), F32),
            pltpu.VMEM((n_sub, dr // LANES, n_grp, LANES), F32),
            pltpu.VMEM((n_sub, dr // LANES, n_grp, LANES), F32),
            pltpu.VMEM((n_sub, dr // LANES, n_grp, LANES), F32),
            pltpu.VMEM((1, dr), F32),
        ],
        compiler_params=_params(2),
        name="rglru",
    )(x, mod_l, norm_g_l, w_in.astype(BF16), conv_w, row(conv_b), w_r.astype(BF16), row(b_r),
      w_i.astype(BF16), row(b_i), row(lam), w_out.astype(BF16))


def kernel(x, c, mod_w, mod_b, norm_g, ffn_w_gu, ffn_w_down, sb_w_qkv, sb_w_o, lru_w_in, lru_conv_w,
           lru_conv_b, lru_w_r, lru_b_r, lru_w_i, lru_b_i, lru_lambda, lru_w_out, final_norm_g):
    depth = mod_w.shape[0]
    b, s, d = x.shape
    mod = _mod_call(c, mod_w, mod_b).reshape(depth, b, N_SUB * 3, d)
    w_gu, w_down = ffn_w_gu[0, 0].astype(BF16), ffn_w_down[0, 0].astype(BF16)
    for layer in range(depth):
        mod_l, ng = mod[layer], norm_g[layer]
        last = layer == depth - 1
        nxt_b = (ffn_w_gu, ffn_w_down, layer, 1)
        nxt_a = None if last else (ffn_w_gu, ffn_w_down, layer + 1, 0)
        j = layer // 2
        if layer % 2 == 0:
            x, qt, k, vt, w_gu, w_down = _ffn_call(x, mod_l, ng, w_gu, w_down, sub=0, w_qkv=sb_w_qkv[j],
                                                   next_weights=nxt_b)
            ot = _attn_call(qt, k, vt)
            res = _ffn_call(x, mod_l, ng, w_gu, w_down, sub=2, final_g=final_norm_g if last else None,
                            attn_out=ot, w_o=sb_w_o[j], next_weights=nxt_a)
        else:
            x, w_gu, w_down = _ffn_call(x, mod_l, ng, w_gu, w_down, sub=0, next_weights=nxt_b)
            x = _lru_call(x, mod_l, ng, lru_w_in[j], lru_conv_w[j], lru_conv_b[j], lru_w_r[j],
                          lru_b_r[j], lru_w_i[j], lru_b_i[j], lru_lambda[j], lru_w_out[j], sub=1)
            res = _ffn_call(x, mod_l, ng, w_gu, w_down, sub=2, final_g=final_norm_g if last else None,
                            next_weights=nxt_a)
        x, w_gu, w_down = (res, None, None) if last else res
    return x
```

```python
import functools

import jax
import jax.numpy as jnp
import numpy as np
from jax import lax
from jax.experimental import pallas as pl
from jax.experimental.pallas import tpu as pltpu

SB_HEADS = 16
LRU_C = 8.0
MACARON_W = 0.5
EPS = 1e-6
N_SUB = 3

LANES = 128
SUBLANES = 8
BF16_SUBLANES = 2 * SUBLANES
MXU_TILE = 256
VMEM_LIMIT_BYTES = 56 * 2 ** 20

TOKEN_TILE = 512
LRU_SUB_TILES = 4
MOD_COL_TILE = 2304

ATTN_HEADS_PER_STEP = 8
NEG_BIG = -1e30
LOG2_E = 1.4426950408889634
TINY = 1e-30

F32 = jnp.float32
BF16 = jnp.bfloat16


def _params(n_grid_axes):
    return pltpu.CompilerParams(
        dimension_semantics=("arbitrary",) * n_grid_axes,
        vmem_limit_bytes=VMEM_LIMIT_BYTES,
    )


def _resident(block_shape, index_map):
    return pl.BlockSpec(block_shape, index_map, pipeline_mode=pl.Buffered(1))


def _norm_mod(x, g, shift, scale):
    inv = lax.rsqrt(jnp.mean(x * x, axis=-1, keepdims=True) + EPS)
    return (x * inv) * (g * (1.0 + scale)) + shift


def _mod_rows(mod_ref, sub):
    r = N_SUB * sub
    return mod_ref[r:r + 1, :], mod_ref[r + 1:r + 2, :], mod_ref[r + 2:r + 3, :]


def _mod_kernel(c_ref, w_ref, b_ref, o_ref):
    c = c_ref[...]
    ca = c * jax.nn.sigmoid(c)
    ca2 = jnp.concatenate([ca, ca], axis=0).astype(BF16)
    y = jnp.dot(ca2, w_ref[...].astype(BF16), preferred_element_type=F32)
    o_ref[...] = y[:c.shape[0]] + b_ref[...]


def _mod_call(c, mod_w, mod_b):
    depth, d, n = mod_w.shape
    b = c.shape[0]
    tn = MOD_COL_TILE
    assert n % tn == 0
    return pl.pallas_call(
        _mod_kernel,
        out_shape=jax.ShapeDtypeStruct((depth, b, n), F32),
        grid=(depth, n // tn),
        in_specs=[
            pl.BlockSpec((b, d), lambda l, j: (0, 0)),
            pl.BlockSpec((None, d, tn), lambda l, j: (l, 0, j)),
            pl.BlockSpec((None, 1, tn), lambda l, j: (l, 0, j)),
        ],
        out_specs=pl.BlockSpec((None, b, tn), lambda l, j: (l, 0, j)),
        compiler_params=_params(2),
        name="mod",
    )(c, mod_w, mod_b.reshape(depth, 1, n))


def _ffn_kernel(x_ref, mod_ref, g_ref, wgu_ref, wd_ref, *rest, sub, final, mix_sub, pre_oproj, post_qkv,
                cast_next, scale_q):
    rest = list(rest)
    if pre_oproj:
        ot_ref, wo_ref = rest[:2]
        rest = rest[2:]
    if post_qkv:
        wqkv_ref = rest.pop(0)
    if final:
        fg_ref = rest.pop(0)
    if cast_next:
        ngu_ref, nwd_ref = rest[:2]
        rest = rest[2:]
        ngu_out_ref, nwd_out_ref = rest[-2:]
        rest = rest[:-2]
        ngu_out_ref[...] = ngu_ref[...].astype(BF16)
        nwd_out_ref[...] = nwd_ref[...].astype(BF16)
    o_ref = rest.pop(0)
    if pre_oproj:
        x = _oproj_residual(ot_ref, x_ref, mod_ref, wo_ref, mix_sub)
    else:
        x = x_ref[...]
    d_ff = wd_ref.shape[0]
    tf = MXU_TILE
    shift, scale, gate = _mod_rows(mod_ref, sub)
    h = _norm_mod(x, g_ref[sub:sub + 1, :], shift, scale).astype(BF16)
    acts = []
    for c in range(d_ff // tf):
        g = jnp.dot(h, wgu_ref[:, c * tf:(c + 1) * tf], preferred_element_type=F32)
        u = jnp.dot(h, wgu_ref[:, d_ff + c * tf:d_ff + (c + 1) * tf], preferred_element_type=F32)
        acts.append((g * jax.nn.sigmoid(g) * u).astype(BF16))
    y = jnp.dot(jnp.concatenate(acts, axis=1), wd_ref[...], preferred_element_type=F32)
    out = x + (MACARON_W * (1.0 + gate)) * y
    if final:
        inv = lax.rsqrt(jnp.mean(out * out, axis=-1, keepdims=True) + EPS)
        out = (out * inv) * fg_ref[...]
    o_ref[...] = out
    if post_qkv:
        _qkv_emit(out, mod_ref, g_ref, wqkv_ref, *rest, sub=mix_sub, scale_q=scale_q)


def _oproj_residual(ot_ref, x_ref, mod_ref, w_ref, sub):
    _, _, gate = _mod_rows(mod_ref, sub)
    tt = ot_ref.shape[-1]
    parts = []
    for j in range(ot_ref.shape[0]):
        y = lax.dot_general(ot_ref[j], w_ref[...], (((0,), (0,)), ((), ())),
                            preferred_element_type=F32)
        parts.append(x_ref[j * tt:(j + 1) * tt, :] + (1.0 + gate) * y)
    return jnp.concatenate(parts, axis=0)


def _qkv_emit(x, mod_ref, g_ref, w_ref, qt_ref, k_ref, vt_ref, *, sub, scale_q):
    d = x.shape[-1]
    shift, scale, _ = _mod_rows(mod_ref, sub)
    h = _norm_mod(x, g_ref[sub:sub + 1, :], shift, scale).astype(BF16)
    q = jnp.dot(h, w_ref[:, 0:d], preferred_element_type=F32) * scale_q
    k_ref[...] = jnp.dot(h, w_ref[:, d:2 * d], preferred_element_type=F32).astype(BF16)
    v = jnp.dot(h, w_ref[:, 2 * d:3 * d], preferred_element_type=F32)
    tt = vt_ref.shape[-1]
    for j in range(vt_ref.shape[0]):
        qt_ref[j] = q[j * tt:(j + 1) * tt, :].T.astype(BF16)
        vt_ref[j] = v[j * tt:(j + 1) * tt, :].T.astype(BF16)


def _ffn_call(x, mod_l, norm_g_l, w_gu, w_down, sub, final_g=None, attn_out=None, w_o=None, w_qkv=None,
              next_weights=None, mix_sub=1, tt=MXU_TILE):
    b, s, d = x.shape
    d_ff = w_down.shape[0]
    tm = TOKEN_TILE
    assert s % tm == 0 and d_ff % MXU_TILE == 0 and tm % tt == 0
    final, pre_oproj, post_qkv = final_g is not None, attn_out is not None, w_qkv is not None
    cast_next = next_weights is not None
    n_steps = b * (s // tm)
    tile_spec = pl.BlockSpec((None, tm, d), lambda bi, i: (bi, i, 0))
    tiled_t_spec = pl.BlockSpec((None, tm // tt, d, tt), lambda bi, i: (bi, i, 0, 0))
    in_specs = [
        tile_spec,
        pl.BlockSpec((None, N_SUB * 3, d), lambda bi, i: (bi, 0, 0)),
        pl.BlockSpec((N_SUB, d), lambda bi, i: (0, 0)),
        _resident((d, 2 * d_ff), lambda bi, i: (0, 0)),
        _resident((d_ff, d), lambda bi, i: (0, 0)),
    ]
    args = [x, mod_l, norm_g_l, w_gu, w_down]
    if pre_oproj:
        in_specs += [tiled_t_spec, _resident((d, d), lambda bi, i: (0, 0))]
        args += [attn_out, w_o.astype(BF16)]
    if post_qkv:
        in_specs.append(_resident((d, 3 * d), lambda bi, i: (0, 0)))
        args.append(w_qkv.astype(BF16))
    if final:
        in_specs.append(pl.BlockSpec((1, d), lambda bi, i: (0, 0)))
        args.append(final_g.reshape(1, d))
    out_shape = [jax.ShapeDtypeStruct((b, s, d), F32)]
    out_specs = [tile_spec]
    if post_qkv:
        tiled_t = jax.ShapeDtypeStruct((b, s // tt, d, tt), BF16)
        out_shape += [tiled_t, jax.ShapeDtypeStruct((b, s, d), BF16), tiled_t]
        out_specs += [tiled_t_spec, tile_spec, tiled_t_spec]
    if cast_next:
        w_gu_f32, w_down_f32, nl_, nh_ = next_weights
        flat = lambda bi, i: bi * (s // tm) + i
        for w in (w_gu_f32, w_down_f32):
            rows, cols = w.shape[-2:]
            k = next(k for k in (1, 2, 4, 8)
                     if n_steps % k == 0 and (rows * k) % (BF16_SUBLANES * n_steps) == 0)
            blk = rows * k // n_steps
            in_specs.append(pl.BlockSpec((None, None, blk, cols),
                                         lambda bi, i, k=k: (nl_, nh_, flat(bi, i) // k, 0)))
            args.append(w)
            out_shape.append(jax.ShapeDtypeStruct((rows, cols), BF16))
            out_specs.append(pl.BlockSpec((blk, cols), lambda bi, i, k=k: (flat(bi, i) // k, 0)))
    dh = d // SB_HEADS
    res = pl.pallas_call(
        functools.partial(_ffn_kernel, sub=sub, final=final, mix_sub=mix_sub, pre_oproj=pre_oproj,
                          post_qkv=post_qkv, cast_next=cast_next, scale_q=dh ** -0.5 * LOG2_E),
        out_shape=tuple(out_shape),
        grid=(b, s // tm),
        in_specs=in_specs,
        out_specs=tuple(out_specs),
        compiler_params=_params(2),
        name="ffn" + ("_oproj" if pre_oproj else "") + ("_qkv" if post_qkv else "") + ("_final" if final else ""),
    )(*args)
    return res[0] if len(res) == 1 else tuple(res)


def _attn_tables(n_qblocks):
    qb = [0, 0] + list(range(n_qblocks))
    kb = [0, 0] + list(range(n_qblocks))
    dg = [1, 1] + [1] * n_qblocks
    for i in range(n_qblocks):
        for t in range(1, i + 1):
            qb.append(i)
            kb.append(i - t)
            dg.append(0)
    n_back = 2 + len(qb) % 2
    qb, kb, dg = qb + [0] * n_back, kb + [0] * n_back, dg + [1] * n_back
    return tuple(jnp.asarray(np.asarray(v, np.int32)) for v in (qb, kb, dg))


def _attn_kernel(qb_tab, kb_tab, dg_tab, qt_ref, k_ref, vt_ref, ot_ref,
                 z_scr, lb_scr, later_scr, acc_scr, tail_scr, tail_at_scr, cap_scr, *, dh):
    tq = qt_ref.shape[-1]
    tk = vt_ref.shape[-1]
    hp = 2 * dh
    n_heads = qt_ref.shape[1] // dh
    n_iter = qb_tab.shape[0] - 2
    n_diag = qt_ref.shape[0]
    assert n_diag % 2 == 0 and n_iter % 2 == 0

    kr = lax.broadcasted_iota(jnp.int32, (tk, tq), 0)
    qc = lax.broadcasted_iota(jnp.int32, (tk, tq), 1)
    cap_scr[...] = jnp.where(kr < qc, jnp.inf, NEG_BIG).astype(F32)
    z_scr[1] = jnp.full(z_scr.shape[1:], NEG_BIG, F32)
    lb_scr[...] = jnp.full(lb_scr.shape, NEG_BIG, F32)
    later_scr[...] = jnp.zeros(later_scr.shape, F32)
    acc_scr[...] = jnp.zeros(acc_scr.shape, F32)
    tail_scr[...] = jnp.zeros(tail_scr.shape, F32)
    tail_at_scr[...] = jnp.zeros(tail_at_scr.shape, F32)

    jr = lax.broadcasted_iota(jnp.int32, (tk, tk), 0)
    jc = lax.broadcasted_iota(jnp.int32, (tk, tk), 1)
    later_mat = (jc > jr).astype(BF16)
    frow = lax.broadcasted_iota(jnp.int32, (hp, tq), 0)

    def stages(m, slot, diagonal):
        qb_s, kb_s = qb_tab[m + 2], kb_tab[m + 2]
        qb_m, dg_m = qb_tab[m + 1], dg_tab[m + 1]
        qb_o, kb_o = qb_tab[m], kb_tab[m]
        keep_tail = jnp.where(dg_m == 1, 0.0, 1.0)
        k_all = k_ref[pl.ds(pl.multiple_of(kb_s * tk, tk), tk), :]
        qt_all = qt_ref[qb_s]
        vt_all = vt_ref[kb_o]
        for hd in range(n_heads):
            p, sub = divmod(hd, 2)
            pair = slice(p * hp, (p + 1) * hp)
            w = jnp.exp2(lb_scr[hd] + later_scr[hd]).astype(BF16)
            pv = jnp.dot(vt_all[pair, :], w, preferred_element_type=F32)
            rows = slice(hd * dh, (hd + 1) * dh)
            acc_scr[qb_o, rows, :] += pv[sub * dh:(sub + 1) * dh, :] * jnp.exp2(tail_at_scr[hd])
            zz = z_scr[1 - slot, hd]
            log_beta = jnp.minimum(zz, 0.0) - jnp.log(1.0 + jnp.exp2(-jnp.abs(zz))) * LOG2_E
            log_keep = log_beta - zz
            lb_scr[hd] = log_beta
            later = jnp.dot(later_mat, log_keep.astype(BF16), preferred_element_type=F32)
            later_scr[hd] = later
            tail_at = tail_scr[qb_m, hd] * keep_tail
            tail_at_scr[hd] = tail_at
            tail_scr[qb_m, hd] = tail_at + later[0:1, :] + log_keep[0:1, :]
            qt = qt_all[pair, :]
            qt = jnp.where((frow >= sub * dh) & (frow < (sub + 1) * dh), qt, jnp.zeros_like(qt))
            z = jnp.dot(k_all[:, pair], qt, preferred_element_type=F32)
            z_scr[slot, hd] = jnp.minimum(z, cap_scr[...]) if diagonal else z

    def body(i, carry, diagonal):
        stages(2 * i, 0, diagonal)
        stages(2 * i + 1, 1, diagonal)
        return carry

    lax.fori_loop(0, n_diag // 2, functools.partial(body, diagonal=True), 0)
    lax.fori_loop(n_diag // 2, n_iter // 2, functools.partial(body, diagonal=False), 0)
    ot_ref[...] = acc_scr[...].astype(BF16)


def _attn_call(qt, k, vt):
    b, nq, d, tq = qt.shape
    tk = vt.shape[-1]
    s = k.shape[1]
    dh = d // SB_HEADS
    assert 2 * dh == LANES and tq == tk and nq * tq == s
    hg = ATTN_HEADS_PER_STEP * dh
    assert ATTN_HEADS_PER_STEP % 2 == 0 and d % hg == 0
    tiled_spec = pl.BlockSpec((None, nq, hg, tq), lambda bi, g, *_: (bi, 0, g, 0))
    grid_spec = pltpu.PrefetchScalarGridSpec(
        num_scalar_prefetch=3,
        grid=(b, d // hg),
        in_specs=[tiled_spec, pl.BlockSpec((None, s, hg), lambda bi, g, *_: (bi, 0, g)), tiled_spec],
        out_specs=tiled_spec,
        scratch_shapes=[
            pltpu.VMEM((2, ATTN_HEADS_PER_STEP, tk, tq), F32),
            pltpu.VMEM((ATTN_HEADS_PER_STEP, tk, tq), F32),
            pltpu.VMEM((ATTN_HEADS_PER_STEP, tk, tq), F32),
            pltpu.VMEM((nq, hg, tq), F32),
            pltpu.VMEM((nq, ATTN_HEADS_PER_STEP, 1, tq), F32),
            pltpu.VMEM((ATTN_HEADS_PER_STEP, 1, tq), F32),
            pltpu.VMEM((tk, tq), F32),
        ],
    )
    return pl.pallas_call(
        functools.partial(_attn_kernel, dh=dh),
        out_shape=jax.ShapeDtypeStruct(qt.shape, BF16),
        grid_spec=grid_spec,
        compiler_params=_params(2),
        name="sb_attn",
    )(*_attn_tables(nq), qt, k, vt)


def _lru_kernel(x_ref, mod_ref, g_ref, win_ref, cw_ref, cb_ref, wr_ref, br_ref, wi_ref, bi_ref,
                lam_ref, wout_ref, o_ref, xb_scr, hs_scr, ga_scr, gb_scr, hin_scr, h_scr, *, sub, n_sub):
    ts = x_ref.shape[0]
    dr = cb_ref.shape[-1]
    n_taps = cw_ref.shape[0]
    grp = SUBLANES
    pad = grp
    rows_sub = ts // n_sub
    n_slabs, n_grp = dr // LANES, rows_sub // grp
    first = pad - (n_taps - 1)
    assert wr_ref.shape[-1] == LANES and first >= 0

    @pl.when(pl.program_id(1) == 0)
    def _():
        xb_scr[:, 0:pad, :] = jnp.zeros((n_slabs, pad, LANES), F32)
        h_scr[...] = jnp.zeros_like(h_scr)

    x = x_ref[...]
    shift, scale, gate = _mod_rows(mod_ref, sub)
    h = _norm_mod(x, g_ref[sub:sub + 1, :], shift, scale).astype(BF16)
    nl = -lam_ref[...]
    softplus_nl = jnp.maximum(nl, 0.0) + jnp.log(1.0 + jnp.exp(-jnp.abs(nl)))
    gate_br = [None] * n_sub
    hs_sub = [None] * n_sub
    state = {"h": tuple(h_scr[:, n * LANES:(n + 1) * LANES] for n in range(n_slabs))}

    def x_proj(j):
        xb = jnp.dot(h[j * rows_sub:(j + 1) * rows_sub], win_ref[:, dr:2 * dr], preferred_element_type=F32)
        for n in range(n_slabs):
            xb_scr[n, pad + j * rows_sub:pad + (j + 1) * rows_sub, :] = xb[:, n * LANES:(n + 1) * LANES]

    def gate_proj(j):
        gate_br[j] = jnp.dot(h[j * rows_sub:(j + 1) * rows_sub], win_ref[:, 0:dr],
                             preferred_element_type=F32)

    def middle(j):
        base = j * rows_sub
        cum = []
        for n in range(n_slabs):
            lanes = slice(n * LANES, (n + 1) * LANES)
            views = [xb_scr[n, pl.ds(base + o, n_grp, stride=grp), :] for o in range(first, pad + grp)]
            xc_steps = []
            for s in range(grp):
                acc = cb_ref[:, lanes]
                for k in range(n_taps):
                    acc = acc + views[s + k] * cw_ref[k:k + 1, lanes]
                xc_steps.append(acc)
            xc = jnp.concatenate(xc_steps, axis=0)
            xcb = xc.astype(BF16)
            r = jax.nn.sigmoid(jnp.dot(xcb, wr_ref[n], preferred_element_type=F32) + br_ref[:, lanes])
            ig = jax.nn.sigmoid(jnp.dot(xcb, wi_ref[n], preferred_element_type=F32) + bi_ref[:, lanes])
            a = jnp.exp((-LRU_C) * r * softplus_nl[:, lanes])
            v = 1.0 - a * a
            b = (v * lax.rsqrt(jnp.maximum(v, TINY))) * (ig * xc)
            ca = cb = None
            per_step = []
            for s in range(grp):
                a_s, b_s = a[s * n_grp:(s + 1) * n_grp], b[s * n_grp:(s + 1) * n_grp]
                ca, cb = (a_s, b_s) if s == 0 else (a_s * ca, a_s * cb + b_s)
                per_step.append((ca, cb))
            ga_scr[j, n], gb_scr[j, n] = ca, cb
            cum.append(per_step)
        hcur = list(state["h"])
        for g in range(n_grp):
            for n in range(n_slabs):
                hin_scr[j, n, g:g + 1, :] = hcur[n]
                hcur[n] = ga_scr[j, n, g:g + 1, :] * hcur[n] + gb_scr[j, n, g:g + 1, :]
        state["h"] = tuple(hcur)
        for n in range(n_slabs):
            h_in = hin_scr[j, n]
            for s in range(grp):
                ca, cb = cum[n][s]
                hs_scr[n, pl.ds(base + s, n_grp, stride=grp), :] = ca * h_in + cb
        hs_sub[j] = jnp.concatenate([hs_scr[n, base:base + rows_sub, :] for n in range(n_slabs)], axis=1)

    def out_proj(j):
        rows = slice(j * rows_sub, (j + 1) * rows_sub)
        y = (jax.nn.gelu(gate_br[j]) * hs_sub[j]).astype(BF16)
        out = jnp.dot(y, wout_ref[...], preferred_element_type=F32)
        o_ref[rows, :] = x[rows] + (1.0 + gate) * out

    x_proj(0)
    for j in range(n_sub):
        if j + 1 < n_sub:
            x_proj(j + 1)
        gate_proj(j)
        if j >= 1:
            out_proj(j - 1)
        middle(j)
    out_proj(n_sub - 1)
    for n in range(n_slabs):
        h_scr[:, n * LANES:(n + 1) * LANES] = state["h"][n]
        xb_scr[n, 0:pad, :] = xb_scr[n, ts:ts + pad, :]


def _lru_call(x, mod_l, norm_g_l, w_in, conv_w, conv_b, w_r, b_r, w_i, b_i, lam, w_out, sub):
    b, s, d = x.shape
    dr = w_out.shape[0]
    nb, bw, _ = w_r.shape
    ts, n_sub = TOKEN_TILE, LRU_SUB_TILES
    n_grp = ts // n_sub // SUBLANES
    assert s % ts == 0 and dr == nb * bw and bw == LANES
    row = lambda v: v.reshape(1, dr)
    const2 = lambda bi, i: (0, 0)
    const3 = lambda bi, i: (0, 0, 0)
    return pl.pallas_call(
        functools.partial(_lru_kernel, sub=sub, n_sub=n_sub),
        out_shape=jax.ShapeDtypeStruct((b, s, d), F32),
        grid=(b, s // ts),
        in_specs=[
            pl.BlockSpec((None, ts, d), lambda bi, i: (bi, i, 0)),
            pl.BlockSpec((None, N_SUB * 3, d), lambda bi, i: (bi, 0, 0)),
            pl.BlockSpec((N_SUB, d), const2),
            _resident((d, 2 * dr), const2),
            pl.BlockSpec(conv_w.shape, const2),
            pl.BlockSpec((1, dr), const2),
            _resident((nb, bw, bw), const3),
            pl.BlockSpec((1, dr), const2),
            _resident((nb, bw, bw), const3),
            pl.BlockSpec((1, dr), const2),
            pl.BlockSpec((1, dr), const2),
            _resident((dr, d), const2),
        ],
        out_specs=pl.BlockSpec((None, ts, d), lambda bi, i: (bi, i, 0)),
        scratch_shapes=[
            pltpu.VMEM((dr // LANES, ts + SUBLANES, LANES), F32),
            pltpu.VMEM((dr // LANES, ts, LANES), F32),
            pltpu.VMEM((n_sub, dr // LANES, n_grp, LANES), F32),
            pltpu.VMEM((n_sub, dr // LANES, n_grp, LANES), F32),
            pltpu.VMEM((n_sub, dr // LANES, n_grp, LANES), F32),
            pltpu.VMEM((1, dr), F32),
        ],
        compiler_params=_params(2),
        name="rglru",
    )(x, mod_l, norm_g_l, w_in.astype(BF16), conv_w, row(conv_b), w_r.astype(BF16), row(b_r),
      w_i.astype(BF16), row(b_i), row(lam), w_out.astype(BF16))


def kernel(x, c, mod_w, mod_b, norm_g, ffn_w_gu, ffn_w_down, sb_w_qkv, sb_w_o, lru_w_in, lru_conv_w,
           lru_conv_b, lru_w_r, lru_b_r, lru_w_i, lru_b_i, lru_lambda, lru_w_out, final_norm_g):
    depth = mod_w.shape[0]
    b, s, d = x.shape
    mod = _mod_call(c, mod_w, mod_b).reshape(depth, b, N_SUB * 3, d)
    w_gu, w_down = ffn_w_gu[0, 0].astype(BF16), ffn_w_down[0, 0].astype(BF16)
    for layer in range(depth):
        mod_l, ng = mod[layer], norm_g[layer]
        last = layer == depth - 1
        nxt_b = (ffn_w_gu, ffn_w_down, layer, 1)
        nxt_a = None if last else (ffn_w_gu, ffn_w_down, layer + 1, 0)
        j = layer // 2
        if layer % 2 == 0:
            x, qt, k, vt, w_gu, w_down = _ffn_call(x, mod_l, ng, w_gu, w_down, sub=0, w_qkv=sb_w_qkv[j],
                                                   next_weights=nxt_b)
            ot = _attn_call(qt, k, vt)
            res = _ffn_call(x, mod_l, ng, w_gu, w_down, sub=2, final_g=final_norm_g if last else None,
                            attn_out=ot, w_o=sb_w_o[j], next_weights=nxt_a)
        else:
            x, w_gu, w_down = _ffn_call(x, mod_l, ng, w_gu, w_down, sub=0, next_weights=nxt_b)
            x = _lru_call(x, mod_l, ng, lru_w_in[j], lru_conv_w[j], lru_conv_b[j], lru_w_r[j],
                          lru_b_r[j], lru_w_i[j], lru_b_i[j], lru_lambda[j], lru_w_out[j], sub=1)
            res = _ffn_call(x, mod_l, ng, w_gu, w_down, sub=2, final_g=final_norm_g if last else None,
                            next_weights=nxt_a)
        x, w_gu, w_down = (res, None, None) if last else res
    return x
```

```python
import functools

import jax
import jax.numpy as jnp
import numpy as np
from jax import lax
from jax.experimental import pallas as pl
from jax.experimental.pallas import tpu as pltpu

SB_HEADS = 16
LRU_C = 8.0
MACARON_W = 0.5
EPS = 1e-6
N_SUB = 3

LANES = 128
SUBLANES = 8
BF16_SUBLANES = 2 * SUBLANES
MXU_TILE = 256
VMEM_LIMIT_BYTES = 56 * 2 ** 20

TOKEN_TILE = 512
LRU_SUB_TILES = 4
MOD_COL_TILE = 2304

ATTN_HEADS_PER_STEP = 8
NEG_BIG = -1e30
LOG2_E = 1.4426950408889634
TINY = 1e-30

F32 = jnp.float32
BF16 = jnp.bfloat16


def _params(n_grid_axes):
    return pltpu.CompilerParams(
        dimension_semantics=("arbitrary",) * n_grid_axes,
        vmem_limit_bytes=VMEM_LIMIT_BYTES,
    )


def _resident(block_shape, index_map):
    return pl.BlockSpec(block_shape, index_map, pipeline_mode=pl.Buffered(1))


def _norm_mod(x, g, shift, scale):
    inv = lax.rsqrt(jnp.mean(x * x, axis=-1, keepdims=True) + EPS)
    return (x * inv) * (g * (1.0 + scale)) + shift


def _mod_rows(mod_ref, sub):
    r = N_SUB * sub
    return mod_ref[r:r + 1, :], mod_ref[r + 1:r + 2, :], mod_ref[r + 2:r + 3, :]


def _mod_kernel(c_ref, w_ref, b_ref, o_ref):
    c = c_ref[...]
    ca = c * jax.nn.sigmoid(c)
    ca2 = jnp.concatenate([ca, ca], axis=0).astype(BF16)
    y = jnp.dot(ca2, w_ref[...].astype(BF16), preferred_element_type=F32)
    o_ref[...] = y[:c.shape[0]] + b_ref[...]


def _mod_call(c, mod_w, mod_b):
    depth, d, n = mod_w.shape
    b = c.shape[0]
    tn = MOD_COL_TILE
    assert n % tn == 0
    return pl.pallas_call(
        _mod_kernel,
        out_shape=jax.ShapeDtypeStruct((depth, b, n), F32),
        grid=(depth, n // tn),
        in_specs=[
            pl.BlockSpec((b, d), lambda l, j: (0, 0)),
            pl.BlockSpec((None, d, tn), lambda l, j: (l, 0, j)),
            pl.BlockSpec((None, 1, tn), lambda l, j: (l, 0, j)),
        ],
        out_specs=pl.BlockSpec((None, b, tn), lambda l, j: (l, 0, j)),
        compiler_params=_params(2),
        name="mod",
    )(c, mod_w, mod_b.reshape(depth, 1, n))


def _ffn_kernel(x_ref, mod_ref, g_ref, wgu_ref, wd_ref, *rest, sub, final, mix_sub, pre_oproj, post_qkv,
                cast_next, scale_q):
    rest = list(rest)
    if pre_oproj:
        ot_ref, wo_ref = rest[:2]
        rest = rest[2:]
    if post_qkv:
        wqkv_ref = rest.pop(0)
    if final:
        fg_ref = rest.pop(0)
    if cast_next:
        ngu_ref, nwd_ref = rest[:2]
        rest = rest[2:]
        ngu_out_ref, nwd_out_ref = rest[-2:]
        rest = rest[:-2]
        ngu_out_ref[...] = ngu_ref[...].astype(BF16)
        nwd_out_ref[...] = nwd_ref[...].astype(BF16)
    o_ref = rest.pop(0)
    if pre_oproj:
        x = _oproj_residual(ot_ref, x_ref, mod_ref, wo_ref, mix_sub)
    else:
        x = x_ref[...]
    d_ff = wd_ref.shape[0]
    tf = MXU_TILE
    shift, scale, gate = _mod_rows(mod_ref, sub)
    h = _norm_mod(x, g_ref[sub:sub + 1, :], shift, scale).astype(BF16)
    acts = []
    for c in range(d_ff // tf):
        g = jnp.dot(h, wgu_ref[:, c * tf:(c + 1) * tf], preferred_element_type=F32)
        u = jnp.dot(h, wgu_ref[:, d_ff + c * tf:d_ff + (c + 1) * tf], preferred_element_type=F32)
        acts.append((g * jax.nn.sigmoid(g) * u).astype(BF16))
    y = jnp.dot(jnp.concatenate(acts, axis=1), wd_ref[...], preferred_element_type=F32)
    out = x + (MACARON_W * (1.0 + gate)) * y
    if final:
        inv = lax.rsqrt(jnp.mean(out * out, axis=-1, keepdims=True) + EPS)
        out = (out * inv) * fg_ref[...]
    o_ref[...] = out
    if post_qkv:
        _qkv_emit(out, mod_ref, g_ref, wqkv_ref, *rest, sub=mix_sub, scale_q=scale_q)


def _oproj_residual(ot_ref, x_ref, mod_ref, w_ref, sub):
    _, _, gate = _mod_rows(mod_ref, sub)
    tt = ot_ref.shape[-1]
    parts = []
    for j in range(ot_ref.shape[0]):
        y = lax.dot_general(ot_ref[j], w_ref[...], (((0,), (0,)), ((), ())),
                            preferred_element_type=F32)
        parts.append(x_ref[j * tt:(j + 1) * tt, :] + (1.0 + gate) * y)
    return jnp.concatenate(parts, axis=0)


def _qkv_emit(x, mod_ref, g_ref, w_ref, qt_ref, k_ref, vt_ref, *, sub, scale_q):
    d = x.shape[-1]
    shift, scale, _ = _mod_rows(mod_ref, sub)
    h = _norm_mod(x, g_ref[sub:sub + 1, :], shift, scale).astype(BF16)
    q = jnp.dot(h, w_ref[:, 0:d], preferred_element_type=F32) * scale_q
    k_ref[...] = jnp.dot(h, w_ref[:, d:2 * d], preferred_element_type=F32).astype(BF16)
    v = jnp.dot(h, w_ref[:, 2 * d:3 * d], preferred_element_type=F32)
    tt = vt_ref.shape[-1]
    for j in range(vt_ref.shape[0]):
        qt_ref[j] = q[j * tt:(j + 1) * tt, :].T.astype(BF16)
        vt_ref[j] = v[j * tt:(j + 1) * tt, :].T.astype(BF16)


def _ffn_call(x, mod_l, norm_g_l, w_gu, w_down, sub, final_g=None, attn_out=None, w_o=None, w_qkv=None,
              next_weights=None, mix_sub=1, tt=MXU_TILE):
    b, s, d = x.shape
    d_ff = w_down.shape[0]
    tm = TOKEN_TILE
    assert s % tm == 0 and d_ff % MXU_TILE == 0 and tm % tt == 0
    final, pre_oproj, post_qkv = final_g is not None, attn_out is not None, w_qkv is not None
    cast_next = next_weights is not None
    n_steps = b * (s // tm)
    tile_spec = pl.BlockSpec((None, tm, d), lambda bi, i: (bi, i, 0))
    tiled_t_spec = pl.BlockSpec((None, tm // tt, d, tt), lambda bi, i: (bi, i, 0, 0))
    in_specs = [
        tile_spec,
        pl.BlockSpec((None, N_SUB * 3, d), lambda bi, i: (bi, 0, 0)),
        pl.BlockSpec((N_SUB, d), lambda bi, i: (0, 0)),
        _resident((d, 2 * d_ff), lambda bi, i: (0, 0)),
        _resident((d_ff, d), lambda bi, i: (0, 0)),
    ]
    args = [x, mod_l, norm_g_l, w_gu, w_down]
    if pre_oproj:
        in_specs += [tiled_t_spec, _resident((d, d), lambda bi, i: (0, 0))]
        args += [attn_out, w_o.astype(BF16)]
    if post_qkv:
        in_specs.append(_resident((d, 3 * d), lambda bi, i: (0, 0)))
        args.append(w_qkv.astype(BF16))
    if final:
        in_specs.append(pl.BlockSpec((1, d), lambda bi, i: (0, 0)))
        args.append(final_g.reshape(1, d))
    out_shape = [jax.ShapeDtypeStruct((b, s, d), F32)]
    out_specs = [tile_spec]
    if post_qkv:
        tiled_t = jax.ShapeDtypeStruct((b, s // tt, d, tt), BF16)
        out_shape += [tiled_t, jax.ShapeDtypeStruct((b, s, d), BF16), tiled_t]
        out_specs += [tiled_t_spec, tile_spec, tiled_t_spec]
    if cast_next:
        w_gu_f32, w_down_f32, nl_, nh_ = next_weights
        flat = lambda bi, i: bi * (s // tm) + i
        for w in (w_gu_f32, w_down_f32):
            rows, cols = w.shape[-2:]
            k = next(k for k in (1, 2, 4, 8)
                     if n_steps % k == 0 and (rows * k) % (BF16_SUBLANES * n_steps) == 0)
            blk = rows * k // n_steps
            in_specs.append(pl.BlockSpec((None, None, blk, cols),
                                         lambda bi, i, k=k: (nl_, nh_, flat(bi, i) // k, 0)))
            args.append(w)
            out_shape.append(jax.ShapeDtypeStruct((rows, cols), BF16))
            out_specs.append(pl.BlockSpec((blk, cols), lambda bi, i, k=k: (flat(bi, i) // k, 0)))
    dh = d // SB_HEADS
    res = pl.pallas_call(
        functools.partial(_ffn_kernel, sub=sub, final=final, mix_sub=mix_sub, pre_oproj=pre_oproj,
                          post_qkv=post_qkv, cast_next=cast_next, scale_q=dh ** -0.5 * LOG2_E),
        out_shape=tuple(out_shape),
        grid=(b, s // tm),
        in_specs=in_specs,
        out_specs=tuple(out_specs),
        compiler_params=_params(2),
        name="ffn" + ("_oproj" if pre_oproj else "") + ("_qkv" if post_qkv else "") + ("_final" if final else ""),
    )(*args)
    return res[0] if len(res) == 1 else tuple(res)


def _attn_tables(n_qblocks):
    qb = [0, 0] + list(range(n_qblocks))
    kb = [0, 0] + list(range(n_qblocks))
    dg = [1, 1] + [1] * n_qblocks
    for i in range(n_qblocks):
        for t in range(1, i + 1):
            qb.append(i)
            kb.append(i - t)
            dg.append(0)
    n_back = 2 + len(qb) % 2
    qb, kb, dg = qb + [0] * n_back, kb + [0] * n_back, dg + [1] * n_back
    return tuple(jnp.asarray(np.asarray(v, np.int32)) for v in (qb, kb, dg))


def _attn_kernel(qb_tab, kb_tab, dg_tab, qt_ref, k_ref, vt_ref, ot_ref,
                 z_scr, lb_scr, later_scr, acc_scr, tail_scr, tail_at_scr, cap_scr, *, dh):
    tq = qt_ref.shape[-1]
    tk = vt_ref.shape[-1]
    hp = 2 * dh
    n_heads = qt_ref.shape[1] // dh
    n_iter = qb_tab.shape[0] - 2
    n_diag = qt_ref.shape[0]
    assert n_diag % 2 == 0 and n_iter % 2 == 0

    kr = lax.broadcasted_iota(jnp.int32, (tk, tq), 0)
    qc = lax.broadcasted_iota(jnp.int32, (tk, tq), 1)
    cap_scr[...] = jnp.where(kr < qc, jnp.inf, NEG_BIG).astype(F32)
    z_scr[1] = jnp.full(z_scr.shape[1:], NEG_BIG, F32)
    lb_scr[...] = jnp.full(lb_scr.shape, NEG_BIG, F32)
    later_scr[...] = jnp.zeros(later_scr.shape, F32)
    acc_scr[...] = jnp.zeros(acc_scr.shape, F32)
    tail_scr[...] = jnp.zeros(tail_scr.shape, F32)
    tail_at_scr[...] = jnp.zeros(tail_at_scr.shape, F32)

    jr = lax.broadcasted_iota(jnp.int32, (tk, tk), 0)
    jc = lax.broadcasted_iota(jnp.int32, (tk, tk), 1)
    later_mat = (jc > jr).astype(BF16)
    frow = lax.broadcasted_iota(jnp.int32, (hp, tq), 0)

    def stages(m, slot, diagonal):
        qb_s, kb_s = qb_tab[m + 2], kb_tab[m + 2]
        qb_m, dg_m = qb_tab[m + 1], dg_tab[m + 1]
        qb_o, kb_o = qb_tab[m], kb_tab[m]
        keep_tail = jnp.where(dg_m == 1, 0.0, 1.0)
        k_all = k_ref[pl.ds(pl.multiple_of(kb_s * tk, tk), tk), :]
        qt_all = qt_ref[qb_s]
        vt_all = vt_ref[kb_o]
        for hd in range(n_heads):
            p, sub = divmod(hd, 2)
            pair = slice(p * hp, (p + 1) * hp)
            w = jnp.exp2((lb_scr[hd] + later_scr[hd]).astype(BF16))
            pv = jnp.dot(vt_all[pair, :], w, preferred_element_type=F32)
            rows = slice(hd * dh, (hd + 1) * dh)
            acc_scr[qb_o, rows, :] += pv[sub * dh:(sub + 1) * dh, :] * jnp.exp2(tail_at_scr[hd])
            zz = z_scr[1 - slot, hd]
            log_beta = jnp.minimum(zz, 0.0) - jnp.log(1.0 + jnp.exp2(-jnp.abs(zz))) * LOG2_E
            log_keep = log_beta - zz
            lb_scr[hd] = log_beta
            later = jnp.dot(later_mat, log_keep.astype(BF16), preferred_element_type=F32)
            later_scr[hd] = later
            tail_at = tail_scr[qb_m, hd] * keep_tail
            tail_at_scr[hd] = tail_at
            tail_scr[qb_m, hd] = tail_at + later[0:1, :] + log_keep[0:1, :]
            qt = qt_all[pair, :]
            qt = jnp.where((frow >= sub * dh) & (frow < (sub + 1) * dh), qt, jnp.zeros_like(qt))
            z = jnp.dot(k_all[:, pair], qt, preferred_element_type=F32)
            z_scr[slot, hd] = jnp.minimum(z, cap_scr[...]) if diagonal else z

    def body(i, carry, diagonal):
        stages(2 * i, 0, diagonal)
        stages(2 * i + 1, 1, diagonal)
        return carry

    lax.fori_loop(0, n_diag // 2, functools.partial(body, diagonal=True), 0)
    lax.fori_loop(n_diag // 2, n_iter // 2, functools.partial(body, diagonal=False), 0)
    ot_ref[...] = acc_scr[...].astype(BF16)


def _attn_call(qt, k, vt):
    b, nq, d, tq = qt.shape
    tk = vt.shape[-1]
    s = k.shape[1]
    dh = d // SB_HEADS
    assert 2 * dh == LANES and tq == tk and nq * tq == s
    hg = ATTN_HEADS_PER_STEP * dh
    assert ATTN_HEADS_PER_STEP % 2 == 0 and d % hg == 0
    tiled_spec = pl.BlockSpec((None, nq, hg, tq), lambda bi, g, *_: (bi, 0, g, 0))
    grid_spec = pltpu.PrefetchScalarGridSpec(
        num_scalar_prefetch=3,
        grid=(b, d // hg),
        in_specs=[tiled_spec, pl.BlockSpec((None, s, hg), lambda bi, g, *_: (bi, 0, g)), tiled_spec],
        out_specs=tiled_spec,
        scratch_shapes=[
            pltpu.VMEM((2, ATTN_HEADS_PER_STEP, tk, tq), F32),
            pltpu.VMEM((ATTN_HEADS_PER_STEP, tk, tq), F32),
            pltpu.VMEM((ATTN_HEADS_PER_STEP, tk, tq), F32),
            pltpu.VMEM((nq, hg, tq), F32),
            pltpu.VMEM((nq, ATTN_HEADS_PER_STEP, 1, tq), F32),
            pltpu.VMEM((ATTN_HEADS_PER_STEP, 1, tq), F32),
            pltpu.VMEM((tk, tq), F32),
        ],
    )
    return pl.pallas_call(
        functools.partial(_attn_kernel, dh=dh),
        out_shape=jax.ShapeDtypeStruct(qt.shape, BF16),
        grid_spec=grid_spec,
        compiler_params=_params(2),
        name="sb_attn",
    )(*_attn_tables(nq), qt, k, vt)


def _lru_kernel(x_ref, mod_ref, g_ref, win_ref, cw_ref, cb_ref, wr_ref, br_ref, wi_ref, bi_ref,
                lam_ref, wout_ref, o_ref, xb_scr, hs_scr, ga_scr, gb_scr, hin_scr, h_scr, *, sub, n_sub):
    ts = x_ref.shape[0]
    dr = cb_ref.shape[-1]
    n_taps = cw_ref.shape[0]
    grp = SUBLANES
    pad = grp
    rows_sub = ts // n_sub
    n_slabs, n_grp = dr // LANES, rows_sub // grp
    first = pad - (n_taps - 1)
    assert wr_ref.shape[-1] == LANES and first >= 0

    @pl.when(pl.program_id(1) == 0)
    def _():
        xb_scr[:, 0:pad, :] = jnp.zeros((n_slabs, pad, LANES), F32)
        h_scr[...] = jnp.zeros_like(h_scr)

    x = x_ref[...]
    shift, scale, gate = _mod_rows(mod_ref, sub)
    h = _norm_mod(x, g_ref[sub:sub + 1, :], shift, scale).astype(BF16)
    nl = -lam_ref[...]
    softplus_nl = jnp.maximum(nl, 0.0) + jnp.log(1.0 + jnp.exp(-jnp.abs(nl)))
    gate_br = [None] * n_sub
    hs_sub = [None] * n_sub
    state = {"h": tuple(h_scr[:, n * LANES:(n + 1) * LANES] for n in range(n_slabs))}

    def x_proj(j):
        xb = jnp.dot(h[j * rows_sub:(j + 1) * rows_sub], win_ref[:, dr:2 * dr], preferred_element_type=F32)
        for n in range(n_slabs):
            xb_scr[n, pad + j * rows_sub:pad + (j + 1) * rows_sub, :] = xb[:, n * LANES:(n + 1) * LANES]

    def gate_proj(j):
        gate_br[j] = jnp.dot(h[j * rows_sub:(j + 1) * rows_sub], win_ref[:, 0:dr],
                             preferred_element_type=F32)

    def middle(j):
        base = j * rows_sub
        cum = []
        for n in range(n_slabs):
            lanes = slice(n * LANES, (n + 1) * LANES)
            views = [xb_scr[n, pl.ds(base + o, n_grp, stride=grp), :] for o in range(first, pad + grp)]
            xc_steps = []
            for s in range(grp):
                acc = cb_ref[:, lanes]
                for k in range(n_taps):
                    acc = acc + views[s + k] * cw_ref[k:k + 1, lanes]
                xc_steps.append(acc)
            xc = jnp.concatenate(xc_steps, axis=0)
            xcb = xc.astype(BF16)
            r = jax.nn.sigmoid(jnp.dot(xcb, wr_ref[n], preferred_element_type=F32) + br_ref[:, lanes])
            ig = jax.nn.sigmoid(jnp.dot(xcb, wi_ref[n], preferred_element_type=F32) + bi_ref[:, lanes])
            a = jnp.exp((-LRU_C) * r * softplus_nl[:, lanes])
            v = 1.0 - a * a
            b = (v * lax.rsqrt(jnp.maximum(v, TINY))) * (ig * xc)
            ca = cb = None
            per_step = []
            for s in range(grp):
                a_s, b_s = a[s * n_grp:(s + 1) * n_grp], b[s * n_grp:(s + 1) * n_grp]
                ca, cb = (a_s, b_s) if s == 0 else (a_s * ca, a_s * cb + b_s)
                per_step.append((ca, cb))
            ga_scr[j, n], gb_scr[j, n] = ca, cb
            cum.append(per_step)
        hcur = list(state["h"])
        for g in range(n_grp):
            for n in range(n_slabs):
                hin_scr[j, n, g:g + 1, :] = hcur[n]
                hcur[n] = ga_scr[j, n, g:g + 1, :] * hcur[n] + gb_scr[j, n, g:g + 1, :]
        state["h"] = tuple(hcur)
        for n in range(n_slabs):
            h_in = hin_scr[j, n]
            for s in range(grp):
                ca, cb = cum[n][s]
                hs_scr[n, pl.ds(base + s, n_grp, stride=grp), :] = ca * h_in + cb
        hs_sub[j] = jnp.concatenate([hs_scr[n, base:base + rows_sub, :] for n in range(n_slabs)], axis=1)

    def out_proj(j):
        rows = slice(j * rows_sub, (j + 1) * rows_sub)
        y = (jax.nn.gelu(gate_br[j]) * hs_sub[j]).astype(BF16)
        out = jnp.dot(y, wout_ref[...], preferred_element_type=F32)
        o_ref[rows, :] = x[rows] + (1.0 + gate) * out

    x_proj(0)
    for j in range(n_sub):
        if j + 1 < n_sub:
            x_proj(j + 1)
        gate_proj(j)
        if j >= 1:
            out_proj(j - 1)
        middle(j)
    out_proj(n_sub - 1)
    for n in range(n_slabs):
        h_scr[:, n * LANES:(n + 1) * LANES] = state["h"][n]
        xb_scr[n, 0:pad, :] = xb_scr[n, ts:ts + pad, :]


def _lru_call(x, mod_l, norm_g_l, w_in, conv_w, conv_b, w_r, b_r, w_i, b_i, lam, w_out, sub):
    b, s, d = x.shape
    dr = w_out.shape[0]
    nb, bw, _ = w_r.shape
    ts, n_sub = TOKEN_TILE, LRU_SUB_TILES
    n_grp = ts // n_sub // SUBLANES
    assert s % ts == 0 and dr == nb * bw and bw == LANES
    row = lambda v: v.reshape(1, dr)
    const2 = lambda bi, i: (0, 0)
    const3 = lambda bi, i: (0, 0, 0)
    return pl.pallas_call(
        functools.partial(_lru_kernel, sub=sub, n_sub=n_sub),
        out_shape=jax.ShapeDtypeStruct((b, s, d), F32),
        grid=(b, s // ts),
        in_specs=[
            pl.BlockSpec((None, ts, d), lambda bi, i: (bi, i, 0)),
            pl.BlockSpec((None, N_SUB * 3, d), lambda bi, i: (bi, 0, 0)),
            pl.BlockSpec((N_SUB, d), const2),
            _resident((d, 2 * dr), const2),
            pl.BlockSpec(conv_w.shape, const2),
            pl.BlockSpec((1, dr), const2),
            _resident((nb, bw, bw), const3),
            pl.BlockSpec((1, dr), const2),
            _resident((nb, bw, bw), const3),
            pl.BlockSpec((1, dr), const2),
            pl.BlockSpec((1, dr), const2),
            _resident((dr, d), const2),
        ],
        out_specs=pl.BlockSpec((None, ts, d), lambda bi, i: (bi, i, 0)),
        scratch_shapes=[
            pltpu.VMEM((dr // LANES, ts + SUBLANES, LANES), F32),
            pltpu.VMEM((dr // LANES, ts, LANES), F32),
            pltpu.VMEM((n_sub, dr // LANES, n_grp, LANES), F32),
            pltpu.VMEM((n_sub, dr // LANES, n_grp, LANES), F32),
            pltpu.VMEM((n_sub, dr // LANES, n_grp, LANES), F32),
            pltpu.VMEM((1, dr), F32),
        ],
        compiler_params=_params(2),
        name="rglru",
    )(x, mod_l, norm_g_l, w_in.astype(BF16), conv_w, row(conv_b), w_r.astype(BF16), row(b_r),
      w_i.astype(BF16), row(b_i), row(lam), w_out.astype(BF16))


def kernel(x, c, mod_w, mod_b, norm_g, ffn_w_gu, ffn_w_down, sb_w_qkv, sb_w_o, lru_w_in, lru_conv_w,
           lru_conv_b, lru_w_r, lru_b_r, lru_w_i, lru_b_i, lru_lambda, lru_w_out, final_norm_g):
    depth = mod_w.shape[0]
    b, s, d = x.shape
    mod = _mod_call(c, mod_w, mod_b).reshape(depth, b, N_SUB * 3, d)
    w_gu, w_down = ffn_w_gu[0, 0].astype(BF16), ffn_w_down[0, 0].astype(BF16)
    for layer in range(depth):
        mod_l, ng = mod[layer], norm_g[layer]
        last = layer == depth - 1
        nxt_b = (ffn_w_gu, ffn_w_down, layer, 1)
        nxt_a = None if last else (ffn_w_gu, ffn_w_down, layer + 1, 0)
        j = layer // 2
        if layer % 2 == 0:
            x, qt, k, vt, w_gu, w_down = _ffn_call(x, mod_l, ng, w_gu, w_down, sub=0, w_qkv=sb_w_qkv[j],
                                                   next_weights=nxt_b)
            ot = _attn_call(qt, k, vt)
            res = _ffn_call(x, mod_l, ng, w_gu, w_down, sub=2, final_g=final_norm_g if last else None,
                            attn_out=ot, w_o=sb_w_o[j], next_weights=nxt_a)
        else:
            x, w_gu, w_down = _ffn_call(x, mod_l, ng, w_gu, w_down, sub=0, next_weights=nxt_b)
            x = _lru_call(x, mod_l, ng, lru_w_in[j], lru_conv_w[j], lru_conv_b[j], lru_w_r[j],
                          lru_b_r[j], lru_w_i[j], lru_b_i[j], lru_lambda[j], lru_w_out[j], sub=1)
            res = _ffn_call(x, mod_l, ng, w_gu, w_down, sub=2, final_g=final_norm_g if last else None,
                            next_weights=nxt_a)
        x, w_gu, w_down = (res, None, None) if last else res
    return x
```

```python
import functools

import jax
import jax.numpy as jnp
import numpy as np
from jax import lax
from jax.experimental import pallas as pl
from jax.experimental.pallas import tpu as pltpu

SB_HEADS = 16
LRU_C = 8.0
MACARON_W = 0.5
EPS = 1e-6
N_SUB = 3

LANES = 128
SUBLANES = 8
BF16_SUBLANES = 2 * SUBLANES
MXU_TILE = 256
VMEM_LIMIT_BYTES = 56 * 2 ** 20

TOKEN_TILE = 512
LRU_SUB_TILES = 4
MOD_COL_TILE = 2304

ATTN_HEADS_PER_STEP = 8
NEG_BIG = -(2.0 ** 100)
LOG2_E = 1.4426950408889634
TINY = 1e-30

F32 = jnp.float32
BF16 = jnp.bfloat16


def _params(n_grid_axes):
    return pltpu.CompilerParams(
        dimension_semantics=("arbitrary",) * n_grid_axes,
        vmem_limit_bytes=VMEM_LIMIT_BYTES,
    )


def _resident(block_shape, index_map):
    return pl.BlockSpec(block_shape, index_map, pipeline_mode=pl.Buffered(1))


def _norm_mod(x, g, shift, scale):
    inv = lax.rsqrt(jnp.mean(x * x, axis=-1, keepdims=True) + EPS)
    return (x * inv) * (g * (1.0 + scale)) + shift


def _mod_rows(mod_ref, sub):
    r = N_SUB * sub
    return mod_ref[r:r + 1, :], mod_ref[r + 1:r + 2, :], mod_ref[r + 2:r + 3, :]


def _mod_kernel(c_ref, w_ref, b_ref, o_ref):
    c = c_ref[...]
    ca = c * jax.nn.sigmoid(c)
    ca2 = jnp.concatenate([ca, ca], axis=0).astype(BF16)
    y = jnp.dot(ca2, w_ref[...].astype(BF16), preferred_element_type=F32)
    o_ref[...] = y[:c.shape[0]] + b_ref[...]


def _mod_call(c, mod_w, mod_b):
    depth, d, n = mod_w.shape
    b = c.shape[0]
    tn = MOD_COL_TILE
    assert n % tn == 0
    return pl.pallas_call(
        _mod_kernel,
        out_shape=jax.ShapeDtypeStruct((depth, b, n), F32),
        grid=(depth, n // tn),
        in_specs=[
            pl.BlockSpec((b, d), lambda l, j: (0, 0)),
            pl.BlockSpec((None, d, tn), lambda l, j: (l, 0, j)),
            pl.BlockSpec((None, 1, tn), lambda l, j: (l, 0, j)),
        ],
        out_specs=pl.BlockSpec((None, b, tn), lambda l, j: (l, 0, j)),
        compiler_params=_params(2),
        name="mod",
    )(c, mod_w, mod_b.reshape(depth, 1, n))


def _ffn_kernel(x_ref, mod_ref, g_ref, wgu_ref, wd_ref, *rest, sub, final, mix_sub, pre_oproj, post_qkv,
                cast_next, scale_q):
    rest = list(rest)
    if pre_oproj:
        ot_ref, wo_ref = rest[:2]
        rest = rest[2:]
    if post_qkv:
        wqkv_ref = rest.pop(0)
    if final:
        fg_ref = rest.pop(0)
    if cast_next:
        ngu_ref, nwd_ref = rest[:2]
        rest = rest[2:]
        ngu_out_ref, nwd_out_ref = rest[-2:]
        rest = rest[:-2]
        ngu_out_ref[...] = ngu_ref[...].astype(BF16)
        nwd_out_ref[...] = nwd_ref[...].astype(BF16)
    o_ref = rest.pop(0)
    if pre_oproj:
        x = _oproj_residual(ot_ref, x_ref, mod_ref, wo_ref, mix_sub)
    else:
        x = x_ref[...]
    d_ff = wd_ref.shape[0]
    tf = MXU_TILE
    shift, scale, gate = _mod_rows(mod_ref, sub)
    h = _norm_mod(x, g_ref[sub:sub + 1, :], shift, scale).astype(BF16)
    acts = []
    for c in range(d_ff // tf):
        g = jnp.dot(h, wgu_ref[:, c * tf:(c + 1) * tf], preferred_element_type=F32)
        u = jnp.dot(h, wgu_ref[:, d_ff + c * tf:d_ff + (c + 1) * tf], preferred_element_type=F32)
        acts.append((g * jax.nn.sigmoid(g) * u).astype(BF16))
    y = jnp.dot(jnp.concatenate(acts, axis=1), wd_ref[...], preferred_element_type=F32)
    out = x + (MACARON_W * (1.0 + gate)) * y
    if final:
        inv = lax.rsqrt(jnp.mean(out * out, axis=-1, keepdims=True) + EPS)
        out = (out * inv) * fg_ref[...]
    o_ref[...] = out
    if post_qkv:
        _qkv_emit(out, mod_ref, g_ref, wqkv_ref, *rest, sub=mix_sub, scale_q=scale_q)


def _oproj_residual(ot_ref, x_ref, mod_ref, w_ref, sub):
    _, _, gate = _mod_rows(mod_ref, sub)
    tt = ot_ref.shape[-1]
    parts = []
    for j in range(ot_ref.shape[0]):
        y = lax.dot_general(ot_ref[j], w_ref[...], (((0,), (0,)), ((), ())),
                            preferred_element_type=F32)
        parts.append(x_ref[j * tt:(j + 1) * tt, :] + (1.0 + gate) * y)
    return jnp.concatenate(parts, axis=0)


def _qkv_emit(x, mod_ref, g_ref, w_ref, qt_ref, k_ref, vt_ref, *, sub, scale_q):
    d = x.shape[-1]
    shift, scale, _ = _mod_rows(mod_ref, sub)
    h = _norm_mod(x, g_ref[sub:sub + 1, :], shift, scale).astype(BF16)
    q = jnp.dot(h, w_ref[:, 0:d], preferred_element_type=F32) * scale_q
    k_ref[...] = jnp.dot(h, w_ref[:, d:2 * d], preferred_element_type=F32).astype(BF16)
    v = jnp.dot(h, w_ref[:, 2 * d:3 * d], preferred_element_type=F32)
    tt = vt_ref.shape[-1]
    for j in range(vt_ref.shape[0]):
        qt_ref[j] = q[j * tt:(j + 1) * tt, :].T.astype(BF16)
        vt_ref[j] = v[j * tt:(j + 1) * tt, :].T.astype(BF16)


def _ffn_call(x, mod_l, norm_g_l, w_gu, w_down, sub, final_g=None, attn_out=None, w_o=None, w_qkv=None,
              next_weights=None, mix_sub=1, tt=MXU_TILE):
    b, s, d = x.shape
    d_ff = w_down.shape[0]
    tm = TOKEN_TILE
    assert s % tm == 0 and d_ff % MXU_TILE == 0 and tm % tt == 0
    final, pre_oproj, post_qkv = final_g is not None, attn_out is not None, w_qkv is not None
    cast_next = next_weights is not None
    n_steps = b * (s // tm)
    tile_spec = pl.BlockSpec((None, tm, d), lambda bi, i: (bi, i, 0))
    tiled_t_spec = pl.BlockSpec((None, tm // tt, d, tt), lambda bi, i: (bi, i, 0, 0))
    in_specs = [
        tile_spec,
        pl.BlockSpec((None, N_SUB * 3, d), lambda bi, i: (bi, 0, 0)),
        pl.BlockSpec((N_SUB, d), lambda bi, i: (0, 0)),
        _resident((d, 2 * d_ff), lambda bi, i: (0, 0)),
        _resident((d_ff, d), lambda bi, i: (0, 0)),
    ]
    args = [x, mod_l, norm_g_l, w_gu, w_down]
    if pre_oproj:
        in_specs += [tiled_t_spec, _resident((d, d), lambda bi, i: (0, 0))]
        args += [attn_out, w_o.astype(BF16)]
    if post_qkv:
        in_specs.append(_resident((d, 3 * d), lambda bi, i: (0, 0)))
        args.append(w_qkv.astype(BF16))
    if final:
        in_specs.append(pl.BlockSpec((1, d), lambda bi, i: (0, 0)))
        args.append(final_g.reshape(1, d))
    out_shape = [jax.ShapeDtypeStruct((b, s, d), F32)]
    out_specs = [tile_spec]
    if post_qkv:
        tiled_t = jax.ShapeDtypeStruct((b, s // tt, d, tt), BF16)
        out_shape += [tiled_t, jax.ShapeDtypeStruct((b, s, d), BF16), tiled_t]
        out_specs += [tiled_t_spec, tile_spec, tiled_t_spec]
    if cast_next:
        w_gu_f32, w_down_f32, nl_, nh_ = next_weights
        flat = lambda bi, i: bi * (s // tm) + i
        for w in (w_gu_f32, w_down_f32):
            rows, cols = w.shape[-2:]
            k = next(k for k in (1, 2, 4, 8)
                     if n_steps % k == 0 and (rows * k) % (BF16_SUBLANES * n_steps) == 0)
            blk = rows * k // n_steps
            in_specs.append(pl.BlockSpec((None, None, blk, cols),
                                         lambda bi, i, k=k: (nl_, nh_, flat(bi, i) // k, 0)))
            args.append(w)
            out_shape.append(jax.ShapeDtypeStruct((rows, cols), BF16))
            out_specs.append(pl.BlockSpec((blk, cols), lambda bi, i, k=k: (flat(bi, i) // k, 0)))
    dh = d // SB_HEADS
    res = pl.pallas_call(
        functools.partial(_ffn_kernel, sub=sub, final=final, mix_sub=mix_sub, pre_oproj=pre_oproj,
                          post_qkv=post_qkv, cast_next=cast_next, scale_q=dh ** -0.5 * LOG2_E),
        out_shape=tuple(out_shape),
        grid=(b, s // tm),
        in_specs=in_specs,
        out_specs=tuple(out_specs),
        compiler_params=_params(2),
        name="ffn" + ("_oproj" if pre_oproj else "") + ("_qkv" if post_qkv else "") + ("_final" if final else ""),
    )(*args)
    return res[0] if len(res) == 1 else tuple(res)


def _attn_tables(n_qblocks):
    qb = [0, 0] + list(range(n_qblocks))
    kb = [0, 0] + list(range(n_qblocks))
    dg = [1, 1] + [1] * n_qblocks
    for i in range(n_qblocks):
        for t in range(1, i + 1):
            qb.append(i)
            kb.append(i - t)
            dg.append(0)
    n_back = 2 + len(qb) % 2
    qb, kb, dg = qb + [0] * n_back, kb + [0] * n_back, dg + [1] * n_back
    return tuple(jnp.asarray(np.asarray(v, np.int32)) for v in (qb, kb, dg))


def _attn_kernel(qb_tab, kb_tab, dg_tab, qt_ref, k_ref, vt_ref, ot_ref,
                 z_scr, lb_scr, later_scr, acc_scr, tail_scr, tail_at_scr, cap_scr, *, dh):
    tq = qt_ref.shape[-1]
    tk = vt_ref.shape[-1]
    hp = 2 * dh
    n_heads = qt_ref.shape[1] // dh
    n_iter = qb_tab.shape[0] - 2
    n_diag = qt_ref.shape[0]
    assert n_diag % 2 == 0 and n_iter % 2 == 0

    kr = lax.broadcasted_iota(jnp.int32, (tk, tq), 0)
    qc = lax.broadcasted_iota(jnp.int32, (tk, tq), 1)
    cap_scr[...] = jnp.where(kr < qc, jnp.inf, NEG_BIG).astype(F32)
    z_scr[1] = jnp.full(z_scr.shape[1:], NEG_BIG, F32)
    lb_scr[...] = jnp.full(lb_scr.shape, NEG_BIG, F32)
    later_scr[...] = jnp.zeros(later_scr.shape, F32)
    acc_scr[...] = jnp.zeros(acc_scr.shape, F32)
    tail_scr[...] = jnp.zeros(tail_scr.shape, F32)
    tail_at_scr[...] = jnp.zeros(tail_at_scr.shape, F32)

    jr = lax.broadcasted_iota(jnp.int32, (tk, tk), 0)
    jc = lax.broadcasted_iota(jnp.int32, (tk, tk), 1)
    later_mat = (jc > jr).astype(BF16)
    frow = lax.broadcasted_iota(jnp.int32, (hp, tq), 0)

    def stages(m, slot, diagonal):
        qb_s, kb_s = qb_tab[m + 2], kb_tab[m + 2]
        qb_m, dg_m = qb_tab[m + 1], dg_tab[m + 1]
        qb_o, kb_o = qb_tab[m], kb_tab[m]
        keep_tail = jnp.where(dg_m == 1, 0.0, 1.0)
        k_all = k_ref[pl.ds(pl.multiple_of(kb_s * tk, tk), tk), :]
        qt_all = qt_ref[qb_s]
        vt_all = vt_ref[kb_o]

        def on_live_part(fn, x, dead):
            if not diagonal:
                return fn(x)
            hk, hq = tk // 2, tq // 2
            right = fn(x[hk:, hq:])
            bottom = jnp.concatenate([jnp.full((tk - hk, hq), dead, right.dtype), right], axis=1)
            return jnp.concatenate([fn(x[:hk]), bottom], axis=0)

        for hd in range(n_heads):
            p, sub = divmod(hd, 2)
            pair = slice(p * hp, (p + 1) * hp)
            w = on_live_part(lambda t: jnp.exp2(t).astype(BF16), lb_scr[hd] + later_scr[hd], 0.0)
            pv = jnp.dot(vt_all[pair, :], w, preferred_element_type=F32)
            rows = slice(hd * dh, (hd + 1) * dh)
            acc_scr[qb_o, rows, :] += pv[sub * dh:(sub + 1) * dh, :] * jnp.exp2(tail_at_scr[hd])
            zz = z_scr[1 - slot, hd]
            log_beta = on_live_part(
                lambda s: jnp.minimum(s, 0.0) - jnp.log(1.0 + jnp.exp2(-jnp.abs(s))) * LOG2_E, zz, NEG_BIG)
            log_keep = log_beta - zz
            lb_scr[hd] = log_beta
            later = jnp.dot(later_mat, log_keep.astype(BF16), preferred_element_type=F32)
            later_scr[hd] = later
            tail_at = tail_scr[qb_m, hd] * keep_tail
            tail_at_scr[hd] = tail_at
            tail_scr[qb_m, hd] = tail_at + later[0:1, :] + log_keep[0:1, :]
            qt = qt_all[pair, :]
            qt = jnp.where((frow >= sub * dh) & (frow < (sub + 1) * dh), qt, jnp.zeros_like(qt))
            z = jnp.dot(k_all[:, pair], qt, preferred_element_type=F32)
            z_scr[slot, hd] = jnp.minimum(z, cap_scr[...]) if diagonal else z

    def body(i, carry, diagonal):
        stages(2 * i, 0, diagonal)
        stages(2 * i + 1, 1, diagonal)
        return carry

    lax.fori_loop(0, n_diag // 2, functools.partial(body, diagonal=True), 0)
    lax.fori_loop(n_diag // 2, n_iter // 2, functools.partial(body, diagonal=False), 0)
    ot_ref[...] = acc_scr[...].astype(BF16)


def _attn_call(qt, k, vt):
    b, nq, d, tq = qt.shape
    tk = vt.shape[-1]
    s = k.shape[1]
    dh = d // SB_HEADS
    assert 2 * dh == LANES and tq == tk and nq * tq == s
    hg = ATTN_HEADS_PER_STEP * dh
    assert ATTN_HEADS_PER_STEP % 2 == 0 and d % hg == 0
    tiled_spec = pl.BlockSpec((None, nq, hg, tq), lambda bi, g, *_: (bi, 0, g, 0))
    grid_spec = pltpu.PrefetchScalarGridSpec(
        num_scalar_prefetch=3,
        grid=(b, d // hg),
        in_specs=[tiled_spec, pl.BlockSpec((None, s, hg), lambda bi, g, *_: (bi, 0, g)), tiled_spec],
        out_specs=tiled_spec,
        scratch_shapes=[
            pltpu.VMEM((2, ATTN_HEADS_PER_STEP, tk, tq), F32),
            pltpu.VMEM((ATTN_HEADS_PER_STEP, tk, tq), F32),
            pltpu.VMEM((ATTN_HEADS_PER_STEP, tk, tq), F32),
            pltpu.VMEM((nq, hg, tq), F32),
            pltpu.VMEM((nq, ATTN_HEADS_PER_STEP, 1, tq), F32),
            pltpu.VMEM((ATTN_HEADS_PER_STEP, 1, tq), F32),
            pltpu.VMEM((tk, tq), F32),
        ],
    )
    return pl.pallas_call(
        functools.partial(_attn_kernel, dh=dh),
        out_shape=jax.ShapeDtypeStruct(qt.shape, BF16),
        grid_spec=grid_spec,
        compiler_params=_params(2),
        name="sb_attn",
    )(*_attn_tables(nq), qt, k, vt)


def _lru_kernel(x_ref, mod_ref, g_ref, win_ref, cw_ref, cb_ref, wr_ref, br_ref, wi_ref, bi_ref,
                lam_ref, wout_ref, o_ref, xb_scr, hs_scr, ga_scr, gb_scr, hin_scr, h_scr, *, sub, n_sub):
    ts = x_ref.shape[0]
    dr = cb_ref.shape[-1]
    n_taps = cw_ref.shape[0]
    grp = SUBLANES
    pad = grp
    rows_sub = ts // n_sub
    n_slabs, n_grp = dr // LANES, rows_sub // grp
    first = pad - (n_taps - 1)
    assert wr_ref.shape[-1] == LANES and first >= 0

    @pl.when(pl.program_id(1) == 0)
    def _():
        xb_scr[:, 0:pad, :] = jnp.zeros((n_slabs, pad, LANES), F32)
        h_scr[...] = jnp.zeros_like(h_scr)

    x = x_ref[...]
    shift, scale, gate = _mod_rows(mod_ref, sub)
    h = _norm_mod(x, g_ref[sub:sub + 1, :], shift, scale).astype(BF16)
    nl = -lam_ref[...]
    softplus_nl = jnp.maximum(nl, 0.0) + jnp.log(1.0 + jnp.exp(-jnp.abs(nl)))
    gate_br = [None] * n_sub
    hs_sub = [None] * n_sub
    state = {"h": tuple(h_scr[:, n * LANES:(n + 1) * LANES] for n in range(n_slabs))}

    def x_proj(j):
        xb = jnp.dot(h[j * rows_sub:(j + 1) * rows_sub], win_ref[:, dr:2 * dr], preferred_element_type=F32)
        for n in range(n_slabs):
            xb_scr[n, pad + j * rows_sub:pad + (j + 1) * rows_sub, :] = xb[:, n * LANES:(n + 1) * LANES]

    def gate_proj(j):
        gate_br[j] = jnp.dot(h[j * rows_sub:(j + 1) * rows_sub], win_ref[:, 0:dr],
                             preferred_element_type=F32)

    def middle(j):
        base = j * rows_sub
        cum = []
        for n in range(n_slabs):
            lanes = slice(n * LANES, (n + 1) * LANES)
            views = [xb_scr[n, pl.ds(base + o, n_grp, stride=grp), :] for o in range(first, pad + grp)]
            xc_steps = []
            for s in range(grp):
                acc = cb_ref[:, lanes]
                for k in range(n_taps):
                    acc = acc + views[s + k] * cw_ref[k:k + 1, lanes]
                xc_steps.append(acc)
            xc = jnp.concatenate(xc_steps, axis=0)
            xcb = xc.astype(BF16)
            r = jax.nn.sigmoid(jnp.dot(xcb, wr_ref[n], preferred_element_type=F32) + br_ref[:, lanes])
            ig = jax.nn.sigmoid(jnp.dot(xcb, wi_ref[n], preferred_element_type=F32) + bi_ref[:, lanes])
            a = jnp.exp((-LRU_C) * r * softplus_nl[:, lanes])
            v = 1.0 - a * a
            b = (v * lax.rsqrt(jnp.maximum(v, TINY))) * (ig * xc)
            ca = cb = None
            per_step = []
            for s in range(grp):
                a_s, b_s = a[s * n_grp:(s + 1) * n_grp], b[s * n_grp:(s + 1) * n_grp]
                ca, cb = (a_s, b_s) if s == 0 else (a_s * ca, a_s * cb + b_s)
                per_step.append((ca, cb))
            ga_scr[j, n], gb_scr[j, n] = ca, cb
            cum.append(per_step)
        hcur = list(state["h"])
        for g in range(n_grp):
            for n in range(n_slabs):
                hin_scr[j, n, g:g + 1, :] = hcur[n]
                hcur[n] = ga_scr[j, n, g:g + 1, :] * hcur[n] + gb_scr[j, n, g:g + 1, :]
        state["h"] = tuple(hcur)
        for n in range(n_slabs):
            h_in = hin_scr[j, n]
            for s in range(grp):
                ca, cb = cum[n][s]
                hs_scr[n, pl.ds(base + s, n_grp, stride=grp), :] = ca * h_in + cb
        hs_sub[j] = jnp.concatenate([hs_scr[n, base:base + rows_sub, :] for n in range(n_slabs)], axis=1)

    def out_proj(j):
        rows = slice(j * rows_sub, (j + 1) * rows_sub)
        y = (jax.nn.gelu(gate_br[j]) * hs_sub[j]).astype(BF16)
        out = jnp.dot(y, wout_ref[...], preferred_element_type=F32)
        o_ref[rows, :] = x[rows] + (1.0 + gate) * out

    x_proj(0)
    for j in range(n_sub):
        if j + 1 < n_sub:
            x_proj(j + 1)
        gate_proj(j)
        if j >= 1:
            out_proj(j - 1)
        middle(j)
    out_proj(n_sub - 1)
    for n in range(n_slabs):
        h_scr[:, n * LANES:(n + 1) * LANES] = state["h"][n]
        xb_scr[n, 0:pad, :] = xb_scr[n, ts:ts + pad, :]


def _lru_call(x, mod_l, norm_g_l, w_in, conv_w, conv_b, w_r, b_r, w_i, b_i, lam, w_out, sub):
    b, s, d = x.shape
    dr = w_out.shape[0]
    nb, bw, _ = w_r.shape
    ts, n_sub = TOKEN_TILE, LRU_SUB_TILES
    n_grp = ts // n_sub // SUBLANES
    assert s % ts == 0 and dr == nb * bw and bw == LANES
    row = lambda v: v.reshape(1, dr)
    const2 = lambda bi, i: (0, 0)
    const3 = lambda bi, i: (0, 0, 0)
    return pl.pallas_call(
        functools.partial(_lru_kernel, sub=sub, n_sub=n_sub),
        out_shape=jax.ShapeDtypeStruct((b, s, d), F32),
        grid=(b, s // ts),
        in_specs=[
            pl.BlockSpec((None, ts, d), lambda bi, i: (bi, i, 0)),
            pl.BlockSpec((None, N_SUB * 3, d), lambda bi, i: (bi, 0, 0)),
            pl.BlockSpec((N_SUB, d), const2),
            _resident((d, 2 * dr), const2),
            pl.BlockSpec(conv_w.shape, const2),
            pl.BlockSpec((1, dr), const2),
            _resident((nb, bw, bw), const3),
            pl.BlockSpec((1, dr), const2),
            _resident((nb, bw, bw), const3),
            pl.BlockSpec((1, dr), const2),
            pl.BlockSpec((1, dr), const2),
            _resident((dr, d), const2),
        ],
        out_specs=pl.BlockSpec((None, ts, d), lambda bi, i: (bi, i, 0)),
        scratch_shapes=[
            pltpu.VMEM((dr // LANES, ts + SUBLANES, LANES), F32),
            pltpu.VMEM((dr // LANES, ts, LANES), F32),
            pltpu.VMEM((n_sub, dr // LANES, n_grp, LANES), F32),
            pltpu.VMEM((n_sub, dr // LANES, n_grp, LANES), F32),
            pltpu.VMEM((n_sub, dr // LANES, n_grp, LANES), F32),
            pltpu.VMEM((1, dr), F32),
        ],
        compiler_params=_params(2),
        name="rglru",
    )(x, mod_l, norm_g_l, w_in.astype(BF16), conv_w, row(conv_b), w_r.astype(BF16), row(b_r),
      w_i.astype(BF16), row(b_i), row(lam), w_out.astype(BF16))


def kernel(x, c, mod_w, mod_b, norm_g, ffn_w_gu, ffn_w_down, sb_w_qkv, sb_w_o, lru_w_in, lru_conv_w,
           lru_conv_b, lru_w_r, lru_b_r, lru_w_i, lru_b_i, lru_lambda, lru_w_out, final_norm_g):
    depth = mod_w.shape[0]
    b, s, d = x.shape
    mod = _mod_call(c, mod_w, mod_b).reshape(depth, b, N_SUB * 3, d)
    w_gu, w_down = ffn_w_gu[0, 0].astype(BF16), ffn_w_down[0, 0].astype(BF16)
    for layer in range(depth):
        mod_l, ng = mod[layer], norm_g[layer]
        last = layer == depth - 1
        nxt_b = (ffn_w_gu, ffn_w_down, layer, 1)
        nxt_a = None if last else (ffn_w_gu, ffn_w_down, layer + 1, 0)
        j = layer // 2
        if layer % 2 == 0:
            x, qt, k, vt, w_gu, w_down = _ffn_call(x, mod_l, ng, w_gu, w_down, sub=0, w_qkv=sb_w_qkv[j],
                                                   next_weights=nxt_b)
            ot = _attn_call(qt, k, vt)
            res = _ffn_call(x, mod_l, ng, w_gu, w_down, sub=2, final_g=final_norm_g if last else None,
                            attn_out=ot, w_o=sb_w_o[j], next_weights=nxt_a)
        else:
            x, w_gu, w_down = _ffn_call(x, mod_l, ng, w_gu, w_down, sub=0, next_weights=nxt_b)
            x = _lru_call(x, mod_l, ng, lru_w_in[j], lru_conv_w[j], lru_conv_b[j], lru_w_r[j],
                          lru_b_r[j], lru_w_i[j], lru_b_i[j], lru_lambda[j], lru_w_out[j], sub=1)
            res = _ffn_call(x, mod_l, ng, w_gu, w_down, sub=2, final_g=final_norm_g if last else None,
                            next_weights=nxt_a)
        x, w_gu, w_down = (res, None, None) if last else res
    return x
```
